```python
import jax, jax.numpy as jnp
from jax import lax
import numpy as np

D_MODEL = 1024
BATCH = 16
SEQ = 2048
DEPTH = 1

RWKV_HEADS = 8
RWKV_HEAD_DIM = 64
RWKV_WIDTH = RWKV_HEADS * RWKV_HEAD_DIM
DECAY_LORA = 64
ICLR_LORA = 64
GATE_LORA = 160
RWKV_GN_EPS = 64e-5
L2_EPS = 1e-12
ATT_HEADS = 8
ATT_HEAD_DIM = 64
ATT_WIDTH = ATT_HEADS * ATT_HEAD_DIM
IDX_HEADS = 8
IDX_DIM = 64
TOPK_MAX = 256
Q_BLOCK = 128
ROPE_THETA = 10000.0
N_BRANCH = 2
BRANCH_WIDTH = RWKV_WIDTH
FFN_HIDDEN = 4 * D_MODEL
NORM_EPS = 1e-6

RWKV_IN_SIZES = (RWKV_WIDTH, RWKV_WIDTH, RWKV_WIDTH, DECAY_LORA, ICLR_LORA, GATE_LORA)
RWKV_IN = 3 * RWKV_WIDTH + DECAY_LORA + ICLR_LORA + GATE_LORA
DSA_IN_SIZES = (ATT_WIDTH, ATT_HEAD_DIM, ATT_HEAD_DIM, IDX_HEADS * IDX_DIM, IDX_DIM, IDX_HEADS)
DSA_IN = ATT_WIDTH + 2 * ATT_HEAD_DIM + IDX_HEADS * IDX_DIM + IDX_DIM + IDX_HEADS
GATE_IN = N_BRANCH * D_MODEL
N_IN = RWKV_IN + DSA_IN + GATE_IN

kernel_name = "hybrid_rwkv7_dsa_gated_block"


def split_cols(z, sizes):
    idx = np.cumsum(sizes)[:-1].tolist()
    return jnp.split(z, idx, axis=-1)


def rms_norm(x, g):
    xf = x.astype(jnp.float32)
    y = xf * lax.rsqrt(jnp.mean(xf * xf, axis=-1, keepdims=True) + NORM_EPS)
    return (y * g.astype(jnp.float32)).astype(x.dtype)


def rope_tables(T, d):
    inv = 1.0 / (ROPE_THETA ** (jnp.arange(0, d, 2, dtype=jnp.float32) / d))
    ang = jnp.arange(T, dtype=jnp.float32)[:, None] * inv[None, :]
    return jnp.cos(ang), jnp.sin(ang)


def apply_rope(x, cos, sin):
    d2 = x.shape[-1] // 2
    shape = (1, x.shape[1]) + (1,) * (x.ndim - 3) + (d2,)
    c = cos.reshape(shape)
    s = sin.reshape(shape)
    xf = x.astype(jnp.float32)
    x1, x2 = xf[..., :d2], xf[..., d2:]
    return jnp.concatenate([x1 * c - x2 * s, x2 * c + x1 * s], axis=-1).astype(x.dtype)


def token_shift(z, mu):
    prev = jnp.pad(z[:, :-1], ((0, 0), (1, 0), (0, 0)))
    return z + (prev - z) * mu


def rwkv7_mix(z, mu, decay_bias, w_decay_up, iclr_bias, w_iclr_up, w_gate_up,
              k_k, k_a, r_k, gn_w, gn_b):
    B, T, _ = z.shape
    H, N = RWKV_HEADS, RWKV_HEAD_DIM
    f32 = jnp.float32
    z = token_shift(z, mu)
    r, k, v, wd, ad, gd = split_cols(z, RWKV_IN_SIZES)
    w_log = -jax.nn.softplus(-(decay_bias + jnp.tanh(wd) @ w_decay_up)) - 0.5
    decay = jnp.exp(-jnp.exp(w_log.astype(f32)))
    iclr = jax.nn.sigmoid(iclr_bias + ad @ w_iclr_up)
    gate = jax.nn.sigmoid(gd) @ w_gate_up
    heads = lambda t: t.reshape(B, T, H, N).astype(f32)
    kk = heads(k * k_k)
    kk = kk / jnp.maximum(jnp.sqrt(jnp.sum(kk * kk, axis=-1, keepdims=True)), L2_EPS)
    k = k * (1.0 + (iclr - 1.0) * k_a)
    r_h, k_h, v_h, w_h, a_h = heads(r), heads(k), heads(v), heads(decay), heads(iclr)
    a_vec = -kk
    b_vec = kk * a_h

    def step(S, inp):
        r_t, w_t, k_t, v_t, a_t, b_t = inp
        sa = jnp.einsum('bhvk,bhk->bhv', S, a_t)
        S = (S * w_t[:, :, None, :] + sa[..., None] * b_t[:, :, None, :]
             + v_t[..., None] * k_t[:, :, None, :])
        return S, jnp.einsum('bhvk,bhk->bhv', S, r_t)

    tm = lambda t: jnp.moveaxis(t, 1, 0)
    S0 = jnp.zeros((B, H, N, N), f32)
    _, y = lax.scan(step, S0, (tm(r_h), tm(w_h), tm(k_h), tm(v_h), tm(a_vec), tm(b_vec)))
    y = jnp.moveaxis(y, 0, 1)
    mean = jnp.mean(y, axis=-1, keepdims=True)
    var = jnp.mean(jnp.square(y - mean), axis=-1, keepdims=True)
    y = ((y - mean) * lax.rsqrt(var + RWKV_GN_EPS) * gn_w.astype(f32).reshape(H, N)
         + gn_b.astype(f32).reshape(H, N))
    bonus = jnp.sum(r_h * k_h * r_k.astype(f32).reshape(H, N), axis=-1, keepdims=True) * v_h
    y = (y + bonus).reshape(B, T, RWKV_WIDTH)
    return (y * gate).astype(z.dtype)


def dsa_mix(z, cos, sin):
    B, T, _ = z.shape
    f32 = jnp.float32
    q, k, v, qi, ki, wi = split_cols(z, DSA_IN_SIZES)
    q = apply_rope(q.reshape(B, T, ATT_HEADS, ATT_HEAD_DIM), cos, sin)
    k = apply_rope(k, cos, sin)
    qi = apply_rope(qi.reshape(B, T, IDX_HEADS, IDX_DIM), cos, sin)
    ki = apply_rope(ki, cos, sin)
    wi = wi * (IDX_HEADS ** -0.5 * IDX_DIM ** -0.5)
    topk = min(TOPK_MAX, T // 4)
    nb = T // Q_BLOCK
    blocks = lambda t: jnp.moveaxis(t.reshape((B, nb, Q_BLOCK) + t.shape[2:]), 1, 0)
    key_pos = jnp.arange(T)
    gather = jax.vmap(lambda tab, i: tab[i])

    def one_block(args):
        q_b, qi_b, wi_b, qpos = args
        logits = jnp.einsum('bqhd,bsd->bqhs', qi_b, ki)
        score = jnp.einsum('bqh,bqhs->bqs', wi_b, jax.nn.relu(logits)).astype(f32)
        causal = key_pos[None, :] <= qpos[:, None]
        score = jnp.where(causal[None], score, -jnp.inf)
        _, idx = lax.top_k(score, topk)
        k_sel = gather(k, idx)
        v_sel = gather(v, idx)
        valid = idx <= qpos[None, :, None]
        s = jnp.einsum('bqhd,bqkd->bqhk', q_b, k_sel).astype(f32) * (ATT_HEAD_DIM ** -0.5)
        s = jnp.where(valid[:, :, None, :], s, -jnp.inf)
        p = jax.nn.softmax(s, axis=-1).astype(v_sel.dtype)
        return jnp.einsum('bqhk,bqkd->bqhd', p, v_sel)

    pos_blocks = jnp.arange(T).reshape(nb, Q_BLOCK)
    o = lax.map(one_block, (blocks(q), blocks(qi), blocks(wi), pos_blocks))
    return jnp.moveaxis(o, 0, 1).reshape(B, T, ATT_WIDTH)


def setup_inputs(seed: int = 0) -> dict:
    key = jax.random.key(seed)
    ks = jax.random.split(key, 20)
    f32 = jnp.float32
    L = DEPTH
    nrm = lambda k, shape, scale: jax.random.normal(k, shape, f32) * scale
    return {
        "x": nrm(ks[0], (BATCH, SEQ, D_MODEL), 1.0),
        "g_mix": 1.0 + nrm(ks[1], (L, D_MODEL), 0.02),
        "w_in": nrm(ks[2], (L, D_MODEL, N_IN), D_MODEL ** -0.5),
        "mu_shift": jax.random.uniform(ks[3], (L, RWKV_IN), f32),
        "decay_bias": jax.random.uniform(ks[4], (L, RWKV_WIDTH), f32, -6.0, 1.0),
        "w_decay_up": nrm(ks[5], (L, DECAY_LORA, RWKV_WIDTH), 0.5 * DECAY_LORA ** -0.5),
        "iclr_bias": nrm(ks[6], (L, RWKV_WIDTH), 0.5),
        "w_iclr_up": nrm(ks[7], (L, ICLR_LORA, RWKV_WIDTH), ICLR_LORA ** -0.5),
        "w_gate_up": nrm(ks[8], (L, GATE_LORA, RWKV_WIDTH), GATE_LORA ** -0.5),
        "k_k": 0.85 + nrm(ks[9], (L, RWKV_WIDTH), 0.02),
        "k_a": 1.0 + nrm(ks[10], (L, RWKV_WIDTH), 0.02),
        "r_k": nrm(ks[11], (L, RWKV_WIDTH), 0.1),
        "gn_w": 1.0 + nrm(ks[12], (L, RWKV_WIDTH), 0.02),
        "gn_b": nrm(ks[13], (L, RWKV_WIDTH), 0.01),
        "w_branch": nrm(ks[14], (L, N_BRANCH, BRANCH_WIDTH, D_MODEL), BRANCH_WIDTH ** -0.5),
        "w_out": nrm(ks[15], (L, D_MODEL, D_MODEL), D_MODEL ** -0.5),
        "g_ffn": 1.0 + nrm(ks[16], (L, D_MODEL), 0.02),
        "w_ffn_up": nrm(ks[17], (L, D_MODEL, FFN_HIDDEN), D_MODEL ** -0.5),
        "w_ffn_down": nrm(ks[18], (L, FFN_HIDDEN, D_MODEL), FFN_HIDDEN ** -0.5),
        "g_final": 1.0 + nrm(ks[19], (D_MODEL,), 0.02),
    }


def reference(x, g_mix, w_in, mu_shift, decay_bias, w_decay_up, iclr_bias, w_iclr_up,
              w_gate_up, k_k, k_a, r_k, gn_w, gn_b, w_branch, w_out, g_ffn,
              w_ffn_up, w_ffn_down, g_final):
    B, T, D = x.shape
    cos, sin = rope_tables(T, ATT_HEAD_DIM)
    for l in range(DEPTH):
        h = rms_norm(x, g_mix[l])
        z = h @ w_in[l]
        z_rwkv, z_dsa, z_gate = split_cols(z, (RWKV_IN, DSA_IN, GATE_IN))
        y_a = rwkv7_mix(z_rwkv, mu_shift[l], decay_bias[l], w_decay_up[l], iclr_bias[l],
                        w_iclr_up[l], w_gate_up[l], k_k[l], k_a[l], r_k[l], gn_w[l], gn_b[l])
        y_b = dsa_mix(z_dsa, cos, sin)
        branch = jnp.stack([y_a, y_b], axis=2)
        proj = jnp.einsum('btnc,ncd->btnd', branch, w_branch[l])
        gates = jax.nn.sigmoid(z_gate.reshape(B, T, N_BRANCH, D))
        merged = jnp.sum(gates * proj, axis=2)
        x = x + merged @ w_out[l]
        h2 = rms_norm(x, g_ffn[l])
        x = x + jnp.square(jax.nn.relu(h2 @ w_ffn_up[l])) @ w_ffn_down[l]
    return rms_norm(x, g_final)
```

```python
import functools

import numpy as np
import jax
import jax.numpy as jnp
from jax import lax
from jax.experimental import pallas as pl
from jax.experimental.pallas import tpu as pltpu

F32 = jnp.float32
BF16 = jnp.bfloat16

HEADS = 8
HEAD_DIM = 64
WIDTH = HEADS * HEAD_DIM
DECAY_LORA = 64
ICLR_LORA = 64
GATE_LORA = 160
GATE_LORA_PAD = 256
RWKV_IN = 3 * WIDTH + DECAY_LORA + ICLR_LORA + GATE_LORA
RWKV_IN_PAD = 3 * WIDTH + DECAY_LORA + ICLR_LORA + GATE_LORA_PAD
DSA_IN = WIDTH + 2 * HEAD_DIM + WIDTH + HEAD_DIM + HEADS
DSA_IN_PAD = 2 * WIDTH + 2 * 128
TOPK_MAX = 256
Q_BLOCK = 128
ROPE_THETA = 10000.0
NORM_EPS = 1e-6
RWKV_GN_EPS = 64e-5
L2_EPS = 1e-12

LANES = 128
VMEM_LIMIT = 56 * 1024 * 1024

RWKV_CHUNK = 64
RWKV_BLOCK = 256
KEY_CHUNK = 512
SEARCH_ITERS_MAX = 64
MASK_BIAS = -1e30


def _dot(a, b):
    return jnp.dot(a, b, preferred_element_type=F32)


def _dot_nt(a, b):
    return lax.dot_general(a, b, (((1,), (1,)), ((), ())), preferred_element_type=F32)


def _dot_tn(a, b):
    return lax.dot_general(a, b, (((0,), (0,)), ((), ())), preferred_element_type=F32)


def _split2(x):
    hi = x.astype(BF16)
    lo = (x - hi.astype(F32)).astype(BF16)
    return hi, lo


def _split3(x):
    h1 = x.astype(BF16)
    r1 = x - h1.astype(F32)
    h2 = r1.astype(BF16)
    h3 = (r1 - h2.astype(F32)).astype(BF16)
    return h1, h2, h3


def _mm1(dotfn, a, b):
    return dotfn(a.astype(BF16), b.astype(BF16))


def _mm2(dotfn, a, b):
    ah, al = _split2(a)
    bh, bl = _split2(b)
    return dotfn(ah, bh) + (dotfn(ah, bl) + dotfn(al, bh))


def _mm_exact_lhs(dotfn, a_bf16, b):
    b1, b2, b3 = _split3(b)
    return dotfn(a_bf16, b1) + (dotfn(a_bf16, b2) + dotfn(a_bf16, b3))


def _rms(x, g):
    ms = jnp.mean(x * x, axis=-1, keepdims=True)
    return x * lax.rsqrt(ms + NORM_EPS) * g


def _sigmoid(x):
    return 1.0 / (1.0 + jnp.exp(-x))


def _inproj_kernel(x_ref, g_ref, wr_ref, wd_ref, wg_ref, zr_ref, zd_ref, zg_ref):
    h = _rms(x_ref[...], g_ref[...]).astype(BF16)
    zr_ref[...] = _dot(h, wr_ref[...])
    zd_ref[...] = _dot(h, wd_ref[...])
    zg_ref[...] = _dot(h, wg_ref[...])


def _inproj(x2, g, wr, wd, wg, tm):
    n, d = x2.shape
    const = lambda i: (0, 0)
    row = lambda i: (i, 0)
    return pl.pallas_call(
        _inproj_kernel,
        grid=(n // tm,),
        in_specs=[
            pl.BlockSpec((tm, d), row),
            pl.BlockSpec((1, d), const),
            pl.BlockSpec(wr.shape, const),
            pl.BlockSpec(wd.shape, const),
            pl.BlockSpec(wg.shape, const),
        ],
        out_specs=[
            pl.BlockSpec((tm, wr.shape[1]), row),
            pl.BlockSpec((tm, wd.shape[1]), row),
            pl.BlockSpec((tm, wg.shape[1]), row),
        ],
        out_shape=[
            jax.ShapeDtypeStruct((n, wr.shape[1]), F32),
            jax.ShapeDtypeStruct((n, wd.shape[1]), F32),
            jax.ShapeDtypeStruct((n, wg.shape[1]), F32),
        ],
        compiler_params=pltpu.CompilerParams(
            dimension_semantics=("arbitrary",), vmem_limit_bytes=VMEM_LIMIT),
        name="inproj",
    )(x2, g, wr, wd, wg)


def _neumann_inverse(low, eye):
    t = eye + low
    p = low
    steps = int(np.log2(RWKV_CHUNK)) - 1
    for _ in range(steps):
        p = _mm2(_dot, p, p)
        t = t + _mm2(_dot, p, t)
    return t


def _rwkv_kernel(z_ref, mu_ref, dbias_ref, wdec_ref, ibias_ref, wicl_ref, wgate_ref,
                 kk_ref, ka_ref, rk_ref, gnw_ref, gnb_ref, tri_ref, blk_ref, bd_ref,
                 y_ref,
                 state_ref, carry_ref, rt_ref, at_ref, bt_ref, kt_ref, bh_ref, kh_ref,
                 v_ref, pc_ref, yacc_ref):
    tb = z_ref.shape[0]
    c_len = RWKV_CHUNK
    n_chunks = tb // c_len

    @pl.when(pl.program_id(1) == 0)
    def _():
        state_ref[...] = jnp.zeros_like(state_ref)
        carry_ref[...] = jnp.zeros_like(carry_ref)

    z = z_ref[...]
    prev = pltpu.roll(z, 1, 0)
    first_row = lax.broadcasted_iota(jnp.int32, (tb, 1), 0) == 0
    prev = jnp.where(first_row, carry_ref[0:1, :], prev)
    carry_ref[0:1, :] = z[tb - 1:tb, :]
    zs = z + (prev - z) * mu_ref[...]

    r = zs[:, 0:WIDTH]
    k = zs[:, WIDTH:2 * WIDTH]
    v = zs[:, 2 * WIDTH:3 * WIDTH]
    lora_in = zs[:, 3 * WIDTH:3 * WIDTH + LANES]
    gate_in = zs[:, 3 * WIDTH + LANES:]

    dec_pre = dbias_ref[...] + _mm1(_dot, jnp.tanh(lora_in), wdec_ref[...])
    neg = -dec_pre
    softplus = jnp.maximum(neg, 0.0) + jnp.log1p(jnp.exp(-jnp.abs(neg)))
    w_log = -softplus - 0.5
    logw = -jnp.exp(w_log)
    iclr = _sigmoid(ibias_ref[...] + _mm1(_dot, lora_in, wicl_ref[...]))
    gate = _mm1(_dot, _sigmoid(gate_in), wgate_ref[...])

    bd = bd_ref[...]
    kk = k * kk_ref[...]
    kk_n2 = _mm_exact_lhs(lambda a, b: _dot(b, a), bd, kk * kk)
    kk = kk / jnp.maximum(jnp.sqrt(kk_n2), L2_EPS)
    k2 = k * (1.0 + (iclr - 1.0) * ka_ref[...])
    a = -kk
    b = kk * iclr

    cum = _mm_exact_lhs(_dot, tri_ref[...], logw)
    cum_end = _mm_exact_lhs(_dot, blk_ref[...], logw)
    e_pos = jnp.exp(cum)
    e_neg = jnp.exp(-cum)
    e_tail = jnp.exp(cum_end - cum)
    rt_ref[...] = r * e_pos
    at_ref[...] = a * jnp.exp(cum - logw)
    bt_ref[...] = b * e_neg
    kt_ref[...] = k2 * e_neg
    bh_ref[...] = b * e_tail
    kh_ref[...] = k2 * e_tail
    v_ref[...] = v
    pc_ref[...] = jnp.exp(cum_end)

    ri = lax.broadcasted_iota(jnp.int32, (c_len, c_len), 0)
    ci = lax.broadcasted_iota(jnp.int32, (c_len, c_len), 1)
    strict = ci < ri
    incl = ci <= ri
    eye = (ci == ri).astype(F32)

    def chunk_body(c, carry):
        rows = pl.ds(pl.multiple_of(c * c_len, c_len), c_len)
        for h in range(HEADS):
            cols = slice(h * HEAD_DIM, (h + 1) * HEAD_DIM)
            s0 = state_ref[h]
            at = at_ref[rows, cols]
            rt = rt_ref[rows, cols]
            bt = bt_ref[rows, cols]
            kt = kt_ref[rows, cols]
            vh = v_ref[rows, cols]
            qa = jnp.concatenate([at, rt], axis=0)
            a_b = _mm2(_dot_nt, qa, bt)
            a_k = _mm2(_dot_nt, qa, kt)
            q_s = _mm2(_dot_nt, qa, s0)
            low = jnp.where(strict, a_b[:c_len], 0.0)
            inv = _neumann_inverse(low, eye)
            rhs = q_s[:c_len] + _mm2(_dot, jnp.where(strict, a_k[:c_len], 0.0), vh)
            u = _mm2(_dot, inv, rhs)
            yh = (q_s[c_len:]
                  + _mm2(_dot, jnp.where(incl, a_b[c_len:], 0.0), u)
                  + _mm2(_dot, jnp.where(incl, a_k[c_len:], 0.0), vh))
            pc = pc_ref[pl.ds(pl.multiple_of(c * c_len, c_len), 1), cols]
            state_ref[h] = (s0 * pc
                            + _mm2(_dot_tn, u, bh_ref[rows, cols])
                            + _mm2(_dot_tn, vh, kh_ref[rows, cols]))
            yacc_ref[rows, cols] = yh
        return carry

    lax.fori_loop(0, n_chunks, chunk_body, 0)

    y = yacc_ref[...]
    inv_n = 1.0 / HEAD_DIM
    headsum = lambda t: _mm_exact_lhs(lambda a_, b_: _dot(b_, a_), bd, t)
    mean = headsum(y) * inv_n
    yc = y - mean
    var = headsum(yc * yc) * inv_n
    yn = yc * lax.rsqrt(var + RWKV_GN_EPS) * gnw_ref[...] + gnb_ref[...]
    bonus = headsum(r * k2 * rk_ref[...]) * v
    y_ref[...] = ((yn + bonus) * gate).astype(y_ref.dtype)


def _rwkv(zr, batch, seq, mu, dbias, wdec, ibias, wicl, wgate, kk, ka, rk, gnw, gnb):
    tb = min(RWKV_BLOCK, seq)
    n_t = seq // tb
    idx = np.arange(tb)
    same_chunk = (idx[:, None] // RWKV_CHUNK) == (idx[None, :] // RWKV_CHUNK)
    tri = jnp.asarray(same_chunk & (idx[None, :] <= idx[:, None]), BF16)
    blk = jnp.asarray(same_chunk, BF16)
    hid = np.arange(WIDTH) // HEAD_DIM
    bd = jnp.asarray(hid[:, None] == hid[None, :], BF16)
    const = lambda b, i: (0, 0)
    row = lambda b, i: (b * n_t + i, 0)
    vec = lambda n: pl.BlockSpec((1, n), const)
    full = lambda arr: pl.BlockSpec(arr.shape, const)
    act = pltpu.VMEM((tb, WIDTH), F32)
    return pl.pallas_call(
        _rwkv_kernel,
        grid=(batch, n_t),
        in_specs=[
            pl.BlockSpec((tb, RWKV_IN_PAD), row),
            vec(RWKV_IN_PAD), vec(WIDTH), full(wdec), vec(WIDTH), full(wicl), full(wgate),
            vec(WIDTH), vec(WIDTH), vec(WIDTH), vec(WIDTH), vec(WIDTH),
            full(tri), full(blk), full(bd),
        ],
        out_specs=pl.BlockSpec((tb, WIDTH), row),
        out_shape=jax.ShapeDtypeStruct((batch * seq, WIDTH), BF16),
        scratch_shapes=[
            pltpu.VMEM((HEADS, HEAD_DIM, HEAD_DIM), F32),
            pltpu.VMEM((8, RWKV_IN_PAD), F32),
            act, act, act, act, act, act, act, act, act,
        ],
        compiler_params=pltpu.CompilerParams(
            dimension_semantics=("arbitrary", "arbitrary"), vmem_limit_bytes=VMEM_LIMIT),
        name="rwkv7",
    )(zr, mu, dbias, wdec, ibias, wicl, wgate, kk, ka, rk, gnw, gnb, tri, blk, bd)


def _rope(x, cos, sin_signed):
    n = x.shape[1] // LANES
    lane = lax.broadcasted_iota(jnp.int32, x.shape, 1)
    first_half = (lane % HEAD_DIM) < (HEAD_DIM // 2)
    swapped = jnp.where(first_half,
                        pltpu.roll(x, x.shape[1] - HEAD_DIM // 2, 1),
                        pltpu.roll(x, HEAD_DIM // 2, 1))
    if n > 1:
        cos = jnp.concatenate([cos] * n, axis=1)
        sin_signed = jnp.concatenate([sin_signed] * n, axis=1)
    return x * cos + swapped * sin_signed


def _dsa_kernel(qq_ref, kv_ref, kiw_ref, wq_ref, cosq_ref, sinq_ref, cosk_ref, sink_ref,
                o_ref,
                k_s, v_s, ki_s, score_s, *, topk, idx_scale):
    j = pl.program_id(1)
    tq = qq_ref.shape[0]
    kc = min(KEY_CHUNK, kv_ref.shape[0])
    n_ch = (j * tq + tq + kc - 1) // kc
    groups = kc // LANES

    @pl.when(j == 0)
    def _():
        kv = kv_ref[...]
        kvr = _rope(kv, cosk_ref[...], sink_ref[...])
        k_s[...] = kvr[:, :HEAD_DIM].astype(BF16)
        v_s[...] = kv[:, HEAD_DIM:].astype(BF16)
        kiw = kiw_ref[...]
        ki_s[...] = _rope(kiw, cosk_ref[...], sink_ref[...])[:, :HEAD_DIM].astype(BF16)

    cosq = cosq_ref[...]
    sinq = sinq_ref[...]
    q = (_rope(qq_ref[:, 0:WIDTH], cosq, sinq) * (HEAD_DIM ** -0.5)).astype(BF16)
    qi = _rope(qq_ref[:, WIDTH:2 * WIDTH], cosq, sinq).astype(BF16)
    wi = wq_ref[...] * idx_scale

    qpos = j * tq + lax.broadcasted_iota(jnp.int32, (tq, 1), 0)
    lane_c = lax.broadcasted_iota(jnp.int32, (tq, kc), 1)
    inf = jnp.float32(jnp.inf)

    def lane_fold(x, fn):
        acc = x[:, 0:LANES]
        for g in range(1, groups):
            acc = fn(acc, x[:, g * LANES:(g + 1) * LANES])
        return acc

    def score_body(c, carry):
        rmax, rmin = carry
        cols = pl.ds(pl.multiple_of(c * kc, kc), kc)
        ki = ki_s[cols, :]
        acc = jnp.zeros((tq, kc), F32)
        for h in range(HEADS):
            lg = _dot_nt(qi[:, h * HEAD_DIM:(h + 1) * HEAD_DIM], ki)
            acc = acc + wi[:, HEAD_DIM + h:HEAD_DIM + h + 1] * jnp.maximum(lg, 0.0)
        causal = (c * kc + lane_c) <= qpos
        score_s[:, cols] = jnp.where(causal, acc, -inf)
        rmax = jnp.maximum(rmax, lane_fold(jnp.where(causal, acc, -inf), jnp.maximum))
        rmin = jnp.minimum(rmin, lane_fold(jnp.where(causal, acc, inf), jnp.minimum))
        return rmax, rmin

    rmax, rmin = lax.fori_loop(
        0, n_ch, score_body,
        (jnp.full((tq, LANES), -inf, F32), jnp.full((tq, LANES), inf, F32)))
    rmax = jnp.max(rmax, axis=1, keepdims=True)
    rmin = jnp.min(rmin, axis=1, keepdims=True)

    def count(pred_fn):
        def body(c, acc):
            s = score_s[:, pl.ds(pl.multiple_of(c * kc, kc), kc)]
            return acc + lane_fold(pred_fn(s).astype(F32), jnp.add)
        acc = lax.fori_loop(0, n_ch, body, jnp.zeros((tq, LANES), F32))
        return jnp.sum(acc, axis=1, keepdims=True)

    kf = jnp.float32(topk)
    n_causal = (qpos + 1).astype(F32)
    all_rows = n_causal < kf
    c_max = count(lambda s: s >= rmax)
    top_tied = c_max >= kf
    lo = jnp.where(top_tied, rmax, rmin)
    c_lo = jnp.where(top_tied, c_max, n_causal)
    hi = rmax
    done = all_rows | (c_lo == kf) | (lo == hi)

    def search_cond(st):
        _, _, _, done_f, it = st
        return jnp.logical_and(jnp.min(done_f) < 0.5, it < SEARCH_ITERS_MAX)

    def search_body(st):
        lo, hi, c_lo, done_f, it = st
        active = done_f < 0.5
        mid = lo * 0.5 + hi * 0.5
        c = count(lambda s: s >= mid)
        ge = c >= kf
        stuck = (mid == lo) | (mid == hi)
        lo_n = jnp.where(active & ge, mid, lo)
        hi_n = jnp.where(active & jnp.logical_not(ge), mid, hi)
        c_lo_n = jnp.where(active & ge, c, c_lo)
        done_n = jnp.where(active & ((c_lo_n == kf) | stuck), jnp.float32(1.0), done_f)
        return lo_n, hi_n, c_lo_n, done_n, it + 1

    lo, hi, c_lo, _, _ = lax.while_loop(
        search_cond, search_body,
        (lo, hi, c_lo, done.astype(F32), jnp.int32(0)))
    lo = jnp.where(all_rows, -inf, lo)

    def snap_body(c, acc):
        s = score_s[:, pl.ds(pl.multiple_of(c * kc, kc), kc)]
        return jnp.minimum(acc, lane_fold(jnp.where(s >= lo, s, inf), jnp.minimum))
    thr = jnp.min(lax.fori_loop(0, n_ch, snap_body, jnp.full((tq, LANES), inf, F32)),
                  axis=1, keepdims=True)
    need = kf - count(lambda s: s > thr)

    gi = lax.broadcasted_iota(jnp.int32, (LANES, LANES), 0)
    gj = lax.broadcasted_iota(jnp.int32, (LANES, LANES), 1)
    before = (gi < gj).astype(BF16)

    def mask_body(c, run):
        cols = pl.ds(pl.multiple_of(c * kc, kc), kc)
        s = score_s[:, cols]
        causal = (c * kc + lane_c) <= qpos
        out = []
        for g in range(groups):
            sg = s[:, g * LANES:(g + 1) * LANES]
            eq = (sg == thr).astype(F32)
            rank = run + _dot(eq.astype(BF16), before)
            sel = (sg > thr) | ((sg == thr) & (rank < need))
            out.append(sel)
            run = run + jnp.sum(eq, axis=1, keepdims=True)
        sel = jnp.concatenate(out, axis=1) & causal
        score_s[:, cols] = jnp.where(sel, 0.0, MASK_BIAS)
        return run

    lax.fori_loop(0, n_ch, mask_body, jnp.zeros((tq, 1), F32))

    for h in range(HEADS):
        qh = q[:, h * HEAD_DIM:(h + 1) * HEAD_DIM]

        def att_body(c, carry):
            m, l, acc = carry
            cols = pl.ds(pl.multiple_of(c * kc, kc), kc)
            s = _dot_nt(qh, k_s[cols, :]) + score_s[:, cols]
            m_new = jnp.maximum(m, jnp.max(s, axis=1, keepdims=True))
            alpha = jnp.exp(m - m_new)
            p = jnp.exp(s - m_new)
            l = alpha * l + jnp.sum(p, axis=1, keepdims=True)
            acc = alpha * acc + _dot(p.astype(BF16), v_s[cols, :])
            return m_new, l, acc

        m, l, acc = lax.fori_loop(
            0, n_ch, att_body,
            (jnp.full((tq, 1), MASK_BIAS, F32), jnp.zeros((tq, 1), F32),
             jnp.zeros((tq, HEAD_DIM), F32)))
        o_ref[:, h * HEAD_DIM:(h + 1) * HEAD_DIM] = (acc / l).astype(o_ref.dtype)


def _dsa(zd, batch, seq, cos_t, sin_t):
    nb = seq // Q_BLOCK
    topk = min(TOPK_MAX, seq // 4)
    kc = min(KEY_CHUNK, seq)
    idx_scale = HEADS ** -0.5 * HEAD_DIM ** -0.5
    kernel = functools.partial(_dsa_kernel, topk=topk, idx_scale=idx_scale)
    qcols = 2 * WIDTH // LANES
    return pl.pallas_call(
        kernel,
        grid=(batch, nb),
        in_specs=[
            pl.BlockSpec((Q_BLOCK, 2 * WIDTH), lambda b, j: (b * nb + j, 0)),
            pl.BlockSpec((seq, LANES), lambda b, j: (b, qcols)),
            pl.BlockSpec((seq, LANES), lambda b, j: (b, qcols + 1)),
            pl.BlockSpec((Q_BLOCK, LANES), lambda b, j: (b * nb + j, qcols + 1)),
            pl.BlockSpec((Q_BLOCK, LANES), lambda b, j: (j, 0)),
            pl.BlockSpec((Q_BLOCK, LANES), lambda b, j: (j, 0)),
            pl.BlockSpec((seq, LANES), lambda b, j: (0, 0)),
            pl.BlockSpec((seq, LANES), lambda b, j: (0, 0)),
        ],
        out_specs=pl.BlockSpec((Q_BLOCK, WIDTH), lambda b, j: (b * nb + j, 0)),
        out_shape=jax.ShapeDtypeStruct((batch * seq, WIDTH), BF16),
        scratch_shapes=[
            pltpu.VMEM((seq, HEAD_DIM), BF16),
            pltpu.VMEM((seq, HEAD_DIM), BF16),
            pltpu.VMEM((seq, HEAD_DIM), BF16),
            pltpu.VMEM((Q_BLOCK, ((seq + kc - 1) // kc) * kc), F32),
        ],
        compiler_params=pltpu.CompilerParams(
            dimension_semantics=("arbitrary", "arbitrary"), vmem_limit_bytes=VMEM_LIMIT),
        name="dsa",
    )(zd, zd, zd, zd, cos_t, sin_t, cos_t, sin_t)


def _merge_ffn_kernel(x_ref, ya_ref, yb_ref, zg_ref, wba_ref, wbb_ref, wout_ref, gffn_ref,
                      wup_ref, wdown_ref, gfin_ref, o_ref, *, final, hidden_chunk):
    d = x_ref.shape[1]
    pa = _dot(ya_ref[...], wba_ref[...])
    pb = _dot(yb_ref[...], wbb_ref[...])
    merged = _sigmoid(zg_ref[:, 0:d]) * pa + _sigmoid(zg_ref[:, d:2 * d]) * pb
    x1 = x_ref[...] + _dot(merged.astype(BF16), wout_ref[...])
    h2 = _rms(x1, gffn_ref[...]).astype(BF16)
    x2 = x1
    hidden = wup_ref.shape[1]
    for c in range(hidden // hidden_chunk):
        sl = slice(c * hidden_chunk, (c + 1) * hidden_chunk)
        u = jnp.maximum(_dot(h2, wup_ref[:, sl]), 0.0)
        x2 = x2 + _dot((u * u).astype(BF16), wdown_ref[sl, :])
    o_ref[...] = _rms(x2, gfin_ref[...]) if final else x2


def _merge_ffn(x2, ya, yb, zg, wba, wbb, wout, gffn, wup, wdown, gfin, tm, final):
    n, d = x2.shape
    const = lambda i: (0, 0)
    row = lambda i: (i, 0)
    resident = lambda arr: pl.BlockSpec(arr.shape, const, pipeline_mode=pl.Buffered(1))
    kernel = functools.partial(_merge_ffn_kernel, final=final,
                               hidden_chunk=min(1024, wup.shape[1]))
    return pl.pallas_call(
        kernel,
        grid=(n // tm,),
        in_specs=[
            pl.BlockSpec((tm, d), row),
            pl.BlockSpec((tm, WIDTH), row),
            pl.BlockSpec((tm, WIDTH), row),
            pl.BlockSpec((tm, 2 * d), row),
            resident(wba), resident(wbb), resident(wout),
            pl.BlockSpec((1, d), const),
            resident(wup), resident(wdown),
            pl.BlockSpec((1, d), const),
        ],
        out_specs=pl.BlockSpec((tm, d), row),
        out_shape=jax.ShapeDtypeStruct((n, d), F32),
        compiler_params=pltpu.CompilerParams(
            dimension_semantics=("arbitrary",), vmem_limit_bytes=VMEM_LIMIT),
        name="merge_ffn",
    )(x2, ya, yb, zg, wba, wbb, wout, gffn, wup, wdown, gfin)


def _rope_tables(seq):
    half = HEAD_DIM // 2
    inv = 1.0 / (ROPE_THETA ** (jnp.arange(0, HEAD_DIM, 2, dtype=F32) / HEAD_DIM))
    ang = jnp.arange(seq, dtype=F32)[:, None] * inv[None, :]
    cos, sin = jnp.cos(ang), jnp.sin(ang)
    cos_t = jnp.concatenate([cos, cos] * (LANES // HEAD_DIM), axis=1)
    sin_t = jnp.concatenate([-sin, sin] * (LANES // HEAD_DIM), axis=1)
    assert cos_t.shape == (seq, LANES) and half * 2 == HEAD_DIM
    return cos_t, sin_t


def _pad_cols(a, n):
    return jnp.pad(a, ((0, 0), (0, n - a.shape[1])))


def _pad_rows(a, n):
    return jnp.pad(a, ((0, n - a.shape[0]), (0, 0)))


def kernel(x, g_mix, w_in, mu_shift, decay_bias, w_decay_up, iclr_bias, w_iclr_up, w_gate_up,
           k_k, k_a, r_k, gn_w, gn_b, w_branch, w_out, g_ffn, w_ffn_up, w_ffn_down, g_final):
    batch, seq, d = x.shape
    depth = g_mix.shape[0]
    n = batch * seq
    tm = 256
    assert seq % Q_BLOCK == 0 and seq % RWKV_CHUNK == 0 and n % tm == 0
    cos_t, sin_t = _rope_tables(seq)
    row = lambda a: a.reshape(1, -1)
    xf = x.reshape(n, d)
    for l in range(depth):
        w = w_in[l]
        w_rwkv = _pad_cols(w[:, :RWKV_IN], RWKV_IN_PAD).astype(BF16)
        wd = w[:, RWKV_IN:RWKV_IN + DSA_IN]
        o_k, o_v, o_qi = WIDTH, WIDTH + HEAD_DIM, WIDTH + 2 * HEAD_DIM
        o_ki, o_wi = o_qi + WIDTH, o_qi + WIDTH + HEAD_DIM
        w_dsa = _pad_cols(jnp.concatenate(
            [wd[:, :WIDTH], wd[:, o_qi:o_ki], wd[:, o_k:o_v], wd[:, o_v:o_qi],
             wd[:, o_ki:o_wi], wd[:, o_wi:]], axis=1), DSA_IN_PAD).astype(BF16)
        w_gate = w[:, RWKV_IN + DSA_IN:].astype(BF16)
        zr, zd, zg = _inproj(xf, row(g_mix[l]), w_rwkv, w_dsa, w_gate, tm)

        wdec = _pad_rows(w_decay_up[l], LANES).astype(BF16)
        wicl = jnp.pad(w_iclr_up[l], ((DECAY_LORA, 0), (0, 0))).astype(BF16)
        wgate = _pad_rows(w_gate_up[l], GATE_LORA_PAD).astype(BF16)
        mu = _pad_cols(row(mu_shift[l]), RWKV_IN_PAD)
        ya = _rwkv(zr, batch, seq, mu, row(decay_bias[l]), wdec, row(iclr_bias[l]), wicl,
                   wgate, row(k_k[l]), row(k_a[l]), row(r_k[l]), row(gn_w[l]), row(gn_b[l]))
        yb = _dsa(zd, batch, seq, cos_t, sin_t)

        xf = _merge_ffn(xf, ya, yb, zg, w_branch[l, 0].astype(BF16),
                        w_branch[l, 1].astype(BF16), w_out[l].astype(BF16), row(g_ffn[l]),
                        w_ffn_up[l].astype(BF16), w_ffn_down[l].astype(BF16), row(g_final),
                        tm, final=(l == depth - 1))
    return xf.reshape(batch, seq, d)
```

```python
import functools

import numpy as np
import jax
import jax.numpy as jnp
from jax import lax
from jax.experimental import pallas as pl
from jax.experimental.pallas import tpu as pltpu

F32 = jnp.float32
BF16 = jnp.bfloat16

HEADS = 8
HEAD_DIM = 64
PAIRS = HEADS // 2
WIDTH = HEADS * HEAD_DIM
DECAY_LORA = 64
ICLR_LORA = 64
GATE_LORA = 160
GATE_LORA_PAD = 256
RWKV_IN = 3 * WIDTH + DECAY_LORA + ICLR_LORA + GATE_LORA
RWKV_IN_PAD = 3 * WIDTH + DECAY_LORA + ICLR_LORA + GATE_LORA_PAD
DSA_IN = WIDTH + 2 * HEAD_DIM + WIDTH + HEAD_DIM + HEADS
DSA_IN_PAD = 2 * WIDTH + 2 * 128
TOPK_MAX = 256
Q_BLOCK = 128
ROPE_THETA = 10000.0
NORM_EPS = 1e-6
RWKV_GN_EPS = 64e-5
L2_EPS = 1e-12

LANES = 128
SUBLANES = 8
VMEM_LIMIT = 56 * 1024 * 1024

RWKV_CHUNK = 64
RWKV_BLOCK = 256
KEY_CHUNK = 256
SEARCH_UNROLL = 4
SEARCH_ITERS_MAX = 64
MASK_BIAS = -1e30


def _dot(a, b):
    return jnp.dot(a, b, preferred_element_type=F32)


def _dot_nt(a, b):
    return lax.dot_general(a, b, (((1,), (1,)), ((), ())), preferred_element_type=F32)


def _dot_tn(a, b):
    return lax.dot_general(a, b, (((0,), (0,)), ((), ())), preferred_element_type=F32)


def _split2(x):
    hi = x.astype(BF16)
    lo = (x - hi.astype(F32)).astype(BF16)
    return hi, lo


def _split3(x):
    h1 = x.astype(BF16)
    r1 = x - h1.astype(F32)
    h2 = r1.astype(BF16)
    h3 = (r1 - h2.astype(F32)).astype(BF16)
    return h1, h2, h3


def _rms(x, g):
    ms = jnp.mean(x * x, axis=-1, keepdims=True)
    return x * lax.rsqrt(ms + NORM_EPS) * g


def _sigmoid(x):
    return 1.0 / (1.0 + jnp.exp(-x))


def _swap_halves(x):
    n = x.shape[1]
    lane = lax.broadcasted_iota(jnp.int32, x.shape, 1)
    first = (lane % LANES) < HEAD_DIM
    return jnp.where(first, pltpu.roll(x, n - HEAD_DIM, 1), pltpu.roll(x, HEAD_DIM, 1))


def _inproj_kernel(x_ref, g_ref, wr_ref, wd_ref, wg_ref, zr_ref, zd_ref, zg_ref):
    h = _rms(x_ref[...], g_ref[...]).astype(BF16)
    zr_ref[...] = _dot(h, wr_ref[...])
    zd_ref[...] = _dot(h, wd_ref[...])
    zg_ref[...] = _dot(h, wg_ref[...])


def _inproj(x2, g, wr, wd, wg, tm):
    n, d = x2.shape
    const = lambda i: (0, 0)
    row = lambda i: (i, 0)
    return pl.pallas_call(
        _inproj_kernel,
        grid=(n // tm,),
        in_specs=[
            pl.BlockSpec((tm, d), row),
            pl.BlockSpec((1, d), const),
            pl.BlockSpec(wr.shape, const),
            pl.BlockSpec(wd.shape, const),
            pl.BlockSpec(wg.shape, const),
        ],
        out_specs=[
            pl.BlockSpec((tm, wr.shape[1]), row),
            pl.BlockSpec((tm, wd.shape[1]), row),
            pl.BlockSpec((tm, wg.shape[1]), row),
        ],
        out_shape=[
            jax.ShapeDtypeStruct((n, wr.shape[1]), F32),
            jax.ShapeDtypeStruct((n, wd.shape[1]), F32),
            jax.ShapeDtypeStruct((n, wg.shape[1]), F32),
        ],
        compiler_params=pltpu.CompilerParams(
            dimension_semantics=("arbitrary",), vmem_limit_bytes=VMEM_LIMIT),
        name="inproj",
    )(x2, g, wr, wd, wg)


def _rwkv_kernel(z_ref, mu_ref, dbias_ref, wdec_ref, ibias_ref, wicl_ref, wgate_ref,
                 kk_ref, ka_ref, rk_ref, gnw_ref, gnb_ref, tri_ref, blk_ref, bd_ref,
                 y_ref,
                 state_ref, carry_ref, qa_ref, bk_ref, bkh_ref, vsw_ref, pc_ref, yacc_ref):
    tb = z_ref.shape[0]
    c_len = RWKV_CHUNK
    n_chunks = tb // c_len

    @pl.when(pl.program_id(1) == 0)
    def _():
        state_ref[...] = jnp.zeros_like(state_ref)
        carry_ref[...] = jnp.zeros_like(carry_ref)

    z = z_ref[...]
    prev = pltpu.roll(z, 1, 0)
    first_row = lax.broadcasted_iota(jnp.int32, (tb, 1), 0) == 0
    prev = jnp.where(first_row, carry_ref[0:1, :], prev)
    carry_ref[0:1, :] = z[tb - 1:tb, :]
    zs = z + (prev - z) * mu_ref[...]

    r = zs[:, 0:WIDTH]
    k = zs[:, WIDTH:2 * WIDTH]
    v = zs[:, 2 * WIDTH:3 * WIDTH]
    lora_in = zs[:, 3 * WIDTH:3 * WIDTH + LANES]
    gate_in = zs[:, 3 * WIDTH + LANES:]

    dec_pre = dbias_ref[...] + _dot(jnp.tanh(lora_in).astype(BF16), wdec_ref[...])
    neg = -dec_pre
    softplus = jnp.maximum(neg, 0.0) + jnp.log1p(jnp.exp(-jnp.abs(neg)))
    w_log = -softplus - 0.5
    logw = -jnp.exp(w_log)
    iclr = _sigmoid(ibias_ref[...] + _dot(lora_in.astype(BF16), wicl_ref[...]))
    gate = _dot(_sigmoid(gate_in).astype(BF16), wgate_ref[...])

    bd = bd_ref[...]

    def headsum(t):
        hi, lo = _split2(t)
        return _dot(hi, bd) + _dot(lo, bd)

    def chunk_sum(mat, t):
        hi, lo = _split2(t)
        return _dot(mat, hi) + _dot(mat, lo)

    kk = k * kk_ref[...]
    kk = kk / jnp.maximum(jnp.sqrt(headsum(kk * kk)), L2_EPS)
    k2 = k * (1.0 + (iclr - 1.0) * ka_ref[...])
    a = -kk
    b = kk * iclr

    cum = chunk_sum(tri_ref[...], logw)
    cum_end = chunk_sum(blk_ref[...], logw)
    e_neg = jnp.exp(-cum)
    e_tail = jnp.exp(cum_end - cum)
    at = (a * jnp.exp(cum - logw)).astype(BF16)
    rt = (r * jnp.exp(cum)).astype(BF16)
    bt = (b * e_neg).astype(BF16)
    kt = (k2 * e_neg).astype(BF16)
    bh = (b * e_tail).astype(BF16)
    kh = (k2 * e_tail).astype(BF16)
    for c in range(n_chunks):
        src = slice(c * c_len, (c + 1) * c_len)
        top = slice(2 * c * c_len, (2 * c + 1) * c_len)
        bot = slice((2 * c + 1) * c_len, (2 * c + 2) * c_len)
        qa_ref[top, :] = at[src]
        qa_ref[bot, :] = rt[src]
        bk_ref[top, :] = bt[src]
        bk_ref[bot, :] = kt[src]
        bkh_ref[top, :] = bh[src]
        bkh_ref[bot, :] = kh[src]
    vsw_ref[...] = _swap_halves(v).astype(BF16)
    pc_ref[...] = jnp.exp(cum_end)

    ri = lax.broadcasted_iota(jnp.int32, (c_len, 2 * c_len), 0)
    ci = lax.broadcasted_iota(jnp.int32, (c_len, 2 * c_len), 1) % c_len
    strict = ci < ri
    incl = ci <= ri
    lane = lax.broadcasted_iota(jnp.int32, (1, LANES), 1)
    half = [lane < HEAD_DIM, lane >= HEAD_DIM]
    zero_bf = jnp.zeros((), BF16)
    zero_top = jnp.zeros((c_len, LANES), BF16)
    hs = range(HEADS)

    def chunk_body(c, carry):
        rows2 = pl.ds(pl.multiple_of(c * 2 * c_len, 2 * c_len), 2 * c_len)
        rows = pl.ds(pl.multiple_of(c * c_len, c_len), c_len)
        lanes = [slice((h // 2) * LANES, (h // 2 + 1) * LANES) for h in hs]
        own = [half[h % 2] for h in hs]
        other = [half[1 - h % 2] for h in hs]

        qa = [jnp.where(own[h], qa_ref[rows2, lanes[h]], zero_bf) for h in hs]
        ab = [_dot_nt(qa[h], bk_ref[rows2, lanes[h]]) for h in hs]
        mtop = [jnp.where(strict, ab[h][:c_len], 0.0) for h in hs]
        mbot = [jnp.where(incl, ab[h][c_len:], 0.0).astype(BF16) for h in hs]
        vp = [jnp.where(other[h], vsw_ref[rows, lanes[h]], zero_bf) for h in hs]
        lkv = [_dot(mtop[h].astype(BF16), jnp.concatenate([zero_top, vp[h]], axis=0))
               for h in hs]
        x = [qa[h][:c_len].astype(F32) + lkv[h] for h in hs]
        pw = [mtop[h][:, :c_len].astype(BF16) for h in hs]
        steps = int(np.log2(c_len))
        for j in range(steps):
            x = [x[h] + _dot(pw[h], x[h].astype(BF16)) for h in hs]
            if j + 1 < steps:
                pw = [_dot(pw[h], pw[h]).astype(BF16) for h in hs]
        rhs = [jnp.concatenate([x[h].astype(BF16), vp[h]], axis=0) for h in hs]
        gy = [_dot(mbot[h], rhs[h]) for h in hs]
        mn = [_dot_tn(rhs[h], jnp.where(own[h], bkh_ref[rows2, lanes[h]], zero_bf))
              for h in hs]

        st = [state_ref[p] for p in range(PAIRS)]
        ss = [jnp.concatenate([st[p].astype(BF16)] * 2, axis=0) for p in range(PAIRS)]
        gm = [jnp.where(own[h], qa[h][c_len:].astype(F32) + gy[h], 0.0).astype(BF16) for h in hs]
        yfull = [gy[h] + _dot_nt(gm[h], ss[h // 2]) for h in hs]
        sm = [jnp.where(own[h], st[h // 2], 0.0).astype(BF16) for h in hs]
        upd = [_dot(sm[h], mn[h].astype(BF16))
               + mn[h][(1 - h % 2) * c_len:(2 - h % 2) * c_len] for h in hs]
        for p in range(PAIRS):
            pl_lanes = slice(p * LANES, (p + 1) * LANES)
            pc = pc_ref[pl.ds(pl.multiple_of(c * c_len, c_len), 1), pl_lanes]
            state_ref[p] = st[p] * pc + upd[2 * p] + upd[2 * p + 1]
            yacc_ref[rows, pl_lanes] = jnp.where(half[0], yfull[2 * p + 1], yfull[2 * p])
        return carry

    lax.fori_loop(0, n_chunks, chunk_body, 0)

    y = _swap_halves(yacc_ref[...])
    inv_n = 1.0 / HEAD_DIM
    mean = headsum(y) * inv_n
    yc = y - mean
    var = headsum(yc * yc) * inv_n
    yn = yc * lax.rsqrt(var + RWKV_GN_EPS) * gnw_ref[...] + gnb_ref[...]
    bonus = headsum(r * k2 * rk_ref[...]) * v
    y_ref[...] = ((yn + bonus) * gate).astype(y_ref.dtype)


def _rwkv(zr, batch, seq, mu, dbias, wdec, ibias, wicl, wgate, kk, ka, rk, gnw, gnb):
    tb = min(RWKV_BLOCK, seq)
    n_t = seq // tb
    idx = np.arange(tb)
    same_chunk = (idx[:, None] // RWKV_CHUNK) == (idx[None, :] // RWKV_CHUNK)
    tri = jnp.asarray(same_chunk & (idx[None, :] <= idx[:, None]), BF16)
    blk = jnp.asarray(same_chunk, BF16)
    hid = np.arange(WIDTH) // HEAD_DIM
    bd = jnp.asarray(hid[:, None] == hid[None, :], BF16)
    const = lambda b, i: (0, 0)
    row = lambda b, i: (b * n_t + i, 0)
    vec = lambda n: pl.BlockSpec((1, n), const)
    full = lambda arr: pl.BlockSpec(arr.shape, const)
    return pl.pallas_call(
        _rwkv_kernel,
        grid=(batch, n_t),
        in_specs=[
            pl.BlockSpec((tb, RWKV_IN_PAD), row),
            vec(RWKV_IN_PAD), vec(WIDTH), full(wdec), vec(WIDTH), full(wicl), full(wgate),
            vec(WIDTH), vec(WIDTH), vec(WIDTH), vec(WIDTH), vec(WIDTH),
            full(tri), full(blk), full(bd),
        ],
        out_specs=pl.BlockSpec((tb, WIDTH), row),
        out_shape=jax.ShapeDtypeStruct((batch * seq, WIDTH), BF16),
        scratch_shapes=[
            pltpu.VMEM((PAIRS, HEAD_DIM, LANES), F32),
            pltpu.VMEM((SUBLANES, RWKV_IN_PAD), F32),
            pltpu.VMEM((2 * tb, WIDTH), BF16),
            pltpu.VMEM((2 * tb, WIDTH), BF16),
            pltpu.VMEM((2 * tb, WIDTH), BF16),
            pltpu.VMEM((tb, WIDTH), BF16),
            pltpu.VMEM((tb, WIDTH), F32),
            pltpu.VMEM((tb, WIDTH), F32),
        ],
        compiler_params=pltpu.CompilerParams(
            dimension_semantics=("arbitrary", "arbitrary"), vmem_limit_bytes=VMEM_LIMIT),
        name="rwkv7",
    )(zr, mu, dbias, wdec, ibias, wicl, wgate, kk, ka, rk, gnw, gnb, tri, blk, bd)


def _rope(x, cos, sin_signed):
    n = x.shape[1] // LANES
    lane = lax.broadcasted_iota(jnp.int32, x.shape, 1)
    first_half = (lane % HEAD_DIM) < (HEAD_DIM // 2)
    swapped = jnp.where(first_half,
                        pltpu.roll(x, x.shape[1] - HEAD_DIM // 2, 1),
                        pltpu.roll(x, HEAD_DIM // 2, 1))
    if n > 1:
        cos = jnp.concatenate([cos] * n, axis=1)
        sin_signed = jnp.concatenate([sin_signed] * n, axis=1)
    return x * cos + swapped * sin_signed


def _fold(x, fn):
    rows, n = x.shape
    parts = x.reshape(rows // SUBLANES, SUBLANES, n)
    level = [parts[i] for i in range(rows // SUBLANES)]
    while len(level) > 1:
        nxt = [fn(level[i], level[i + 1]) for i in range(0, len(level) - 1, 2)]
        if len(level) % 2:
            nxt.append(level[-1])
        level = nxt
    return level[0]


def _heads_to_rows(x):
    return jnp.concatenate(
        [x[:, h * HEAD_DIM:(h + 1) * HEAD_DIM] for h in range(HEADS)], axis=0)


def _dsa_kernel(qq_ref, kv_ref, kiw_ref, wq_ref, cosq_ref, sinq_ref, cosk_ref, sink_ref,
                o_ref,
                k_s, ki_s, vt_s, sc_s, s_all, *, topk, idx_scale):
    j = pl.program_id(1)
    tq = qq_ref.shape[0]
    seq = kv_ref.shape[0]
    kc = min(KEY_CHUNK, seq)
    n_ch = (j * tq + tq + kc - 1) // kc
    inf = jnp.float32(jnp.inf)

    @pl.when(j == 0)
    def _():
        kv = kv_ref[...]
        kvr = _rope(kv, cosk_ref[...], sink_ref[...])
        k_s[...] = kvr[:, :HEAD_DIM].astype(BF16)
        kvt = kv.T
        for c in range(seq // kc):
            vt_s[c] = kvt[HEAD_DIM:, c * kc:(c + 1) * kc].astype(BF16)
        kiw = kiw_ref[...]
        ki_s[...] = _rope(kiw, cosk_ref[...], sink_ref[...])[:, :HEAD_DIM].astype(BF16)

    cosq = cosq_ref[...]
    sinq = sinq_ref[...]
    q_all = _heads_to_rows(_rope(qq_ref[:, 0:WIDTH], cosq, sinq)
                           * (HEAD_DIM ** -0.5)).astype(BF16)
    qi_all = _heads_to_rows(_rope(qq_ref[:, WIDTH:2 * WIDTH], cosq, sinq)).astype(BF16)

    sel_r = lax.broadcasted_iota(jnp.int32, (2 * SUBLANES, LANES), 0)
    sel_c = lax.broadcasted_iota(jnp.int32, (2 * SUBLANES, LANES), 1)
    sel = (sel_c == sel_r + HEAD_DIM).astype(BF16)
    w1, w2, w3 = _split3(wq_ref[...])
    w_t = (_dot_nt(sel, w1) + (_dot_nt(sel, w2) + _dot_nt(sel, w3))) * idx_scale

    qpos = j * tq + lax.broadcasted_iota(jnp.int32, (1, tq), 1)
    key_in_chunk = lax.broadcasted_iota(jnp.int32, (kc, tq), 0)

    def chunk_rows(c):
        return pl.ds(pl.multiple_of(c * kc, kc), kc)

    def score_body(c, carry):
        rmax, rmin = carry
        ki = ki_s[chunk_rows(c), :]
        acc = jnp.zeros((kc, tq), F32)
        for p in range(PAIRS):
            lg = _dot_nt(ki, qi_all[2 * p * tq:(2 * p + 2) * tq])
            acc = acc + w_t[2 * p:2 * p + 1] * jnp.maximum(lg[:, :tq], 0.0)
            acc = acc + w_t[2 * p + 1:2 * p + 2] * jnp.maximum(lg[:, tq:], 0.0)
        causal = (c * kc + key_in_chunk) <= qpos
        lo_fill = jnp.where(causal, acc, -inf)
        sc_s[chunk_rows(c), :] = lo_fill
        rmax = jnp.maximum(rmax, _fold(lo_fill, jnp.maximum))
        rmin = jnp.minimum(rmin, _fold(jnp.where(causal, acc, inf), jnp.minimum))
        return rmax, rmin

    rmax, rmin = lax.fori_loop(
        0, n_ch, score_body,
        (jnp.full((SUBLANES, tq), -inf, F32), jnp.full((SUBLANES, tq), inf, F32)))
    rmax = jnp.max(rmax, axis=0, keepdims=True)
    rmin = jnp.min(rmin, axis=0, keepdims=True)

    def count(pred_fn):
        def body(c, acc):
            return acc + _fold(pred_fn(sc_s[chunk_rows(c), :]).astype(F32), jnp.add)
        acc = lax.fori_loop(0, n_ch, body, jnp.zeros((SUBLANES, tq), F32))
        return jnp.sum(acc, axis=0, keepdims=True)

    kf = jnp.float32(topk)
    n_causal = (qpos + 1).astype(F32)
    all_keys = n_causal < kf
    c_max = count(lambda s: s >= rmax)
    top_tied = c_max >= kf
    lo = jnp.where(top_tied, rmax, rmin)
    c_lo = jnp.where(top_tied, c_max, n_causal)
    hi = rmax
    done = all_keys | (c_lo == kf) | (lo == hi)

    def search_cond(st):
        _, _, _, done_f, it = st
        return jnp.logical_and(jnp.min(done_f) < 0.5, it < SEARCH_ITERS_MAX)

    def search_body(st):
        lo, hi, c_lo, done_f, it = st
        for _ in range(SEARCH_UNROLL):
            active = done_f < 0.5
            mid = lo * 0.5 + hi * 0.5
            c = count(lambda s: s >= mid)
            ge = c >= kf
            stuck = (mid == lo) | (mid == hi)
            lo = jnp.where(active & ge, mid, lo)
            hi = jnp.where(active & jnp.logical_not(ge), mid, hi)
            c_lo = jnp.where(active & ge, c, c_lo)
            done_f = jnp.where(active & ((c_lo == kf) | stuck), jnp.float32(1.0), done_f)
        return lo, hi, c_lo, done_f, it + SEARCH_UNROLL

    lo, hi, c_lo, _, _ = lax.while_loop(
        search_cond, search_body, (lo, hi, c_lo, done.astype(F32), jnp.int32(0)))
    lo = jnp.where(all_keys, -inf, lo)
    exact_k = all_keys | (c_lo == kf)
    no_ties = jnp.min(exact_k.astype(F32)) > 0.5

    @pl.when(no_ties)
    def _():
        def body(c, carry):
            s = sc_s[chunk_rows(c), :]
            causal = (c * kc + key_in_chunk) <= qpos
            sc_s[chunk_rows(c), :] = jnp.where((s >= lo) & causal, 0.0, MASK_BIAS)
            return carry
        lax.fori_loop(0, n_ch, body, 0)

    @pl.when(jnp.logical_not(no_ties))
    def _():
        def snap_body(c, acc):
            s = sc_s[chunk_rows(c), :]
            return jnp.minimum(acc, _fold(jnp.where(s >= lo, s, inf), jnp.minimum))
        thr = jnp.min(lax.fori_loop(0, n_ch, snap_body, jnp.full((SUBLANES, tq), inf, F32)),
                      axis=0, keepdims=True)
        need = kf - count(lambda s: s > thr)
        ki_ = lax.broadcasted_iota(jnp.int32, (kc, kc), 0)
        kj_ = lax.broadcasted_iota(jnp.int32, (kc, kc), 1)
        earlier = (kj_ < ki_).astype(BF16)
        ones = jnp.ones((2 * SUBLANES, kc), BF16)

        def body(c, run):
            s = sc_s[chunk_rows(c), :]
            causal = (c * kc + key_in_chunk) <= qpos
            eq = (s == thr).astype(BF16)
            rank = run + _dot(earlier, eq)
            take = (s > thr) | ((s == thr) & (rank < need))
            sc_s[chunk_rows(c), :] = jnp.where(take & causal, 0.0, MASK_BIAS)
            return run + _dot(ones, eq)[0:1]
        lax.fori_loop(0, n_ch, body, jnp.zeros((1, tq), F32))

    def logits_body(c, m_acc):
        s = _dot_nt(k_s[chunk_rows(c), :], q_all)
        s = s + jnp.concatenate([sc_s[chunk_rows(c), :]] * HEADS, axis=1)
        s_all[chunk_rows(c), :] = s
        return jnp.maximum(m_acc, _fold(s, jnp.maximum))

    m = jnp.max(lax.fori_loop(0, n_ch, logits_body,
                              jnp.full((SUBLANES, HEADS * tq), MASK_BIAS, F32)),
                axis=0, keepdims=True)

    def pv_body(c, carry):
        l_acc, acc = carry
        p = jnp.exp(s_all[chunk_rows(c), :] - m)
        l_acc = l_acc + _fold(p, jnp.add)
        acc = acc + _dot(vt_s[c], p.astype(BF16))
        return l_acc, acc

    l_acc, acc = lax.fori_loop(
        0, n_ch, pv_body,
        (jnp.zeros((SUBLANES, HEADS * tq), F32), jnp.zeros((HEAD_DIM, HEADS * tq), F32)))
    out_t = acc / jnp.sum(l_acc, axis=0, keepdims=True)
    for p in range(PAIRS):
        pair = jnp.concatenate([out_t[:, 2 * p * tq:(2 * p + 1) * tq],
                                out_t[:, (2 * p + 1) * tq:(2 * p + 2) * tq]], axis=0)
        o_ref[:, p * LANES:(p + 1) * LANES] = pair.T.astype(o_ref.dtype)


def _dsa(zd, batch, seq, cos_t, sin_t):
    nb = seq // Q_BLOCK
    topk = min(TOPK_MAX, seq // 4)
    kc = min(KEY_CHUNK, seq)
    idx_scale = HEADS ** -0.5 * HEAD_DIM ** -0.5
    kernel = functools.partial(_dsa_kernel, topk=topk, idx_scale=idx_scale)
    qcols = 2 * WIDTH // LANES
    return pl.pallas_call(
        kernel,
        grid=(batch, nb),
        in_specs=[
            pl.BlockSpec((Q_BLOCK, 2 * WIDTH), lambda b, j: (b * nb + j, 0)),
            pl.BlockSpec((seq, LANES), lambda b, j: (b, qcols)),
            pl.BlockSpec((seq, LANES), lambda b, j: (b, qcols + 1)),
            pl.BlockSpec((Q_BLOCK, LANES), lambda b, j: (b * nb + j, qcols + 1)),
            pl.BlockSpec((Q_BLOCK, LANES), lambda b, j: (j, 0)),
            pl.BlockSpec((Q_BLOCK, LANES), lambda b, j: (j, 0)),
            pl.BlockSpec((seq, LANES), lambda b, j: (0, 0)),
            pl.BlockSpec((seq, LANES), lambda b, j: (0, 0)),
        ],
        out_specs=pl.BlockSpec((Q_BLOCK, WIDTH), lambda b, j: (b * nb + j, 0)),
        out_shape=jax.ShapeDtypeStruct((batch * seq, WIDTH), BF16),
        scratch_shapes=[
            pltpu.VMEM((seq, HEAD_DIM), BF16),
            pltpu.VMEM((seq, HEAD_DIM), BF16),
            pltpu.VMEM((seq // kc, HEAD_DIM, kc), BF16),
            pltpu.VMEM((seq, Q_BLOCK), F32),
            pltpu.VMEM((seq, HEADS * Q_BLOCK), F32),
        ],
        compiler_params=pltpu.CompilerParams(
            dimension_semantics=("arbitrary", "arbitrary"), vmem_limit_bytes=VMEM_LIMIT),
        name="dsa",
    )(zd, zd, zd, zd, cos_t, sin_t, cos_t, sin_t)


def _merge_ffn_kernel(x_ref, ya_ref, yb_ref, zg_ref, wba_ref, wbb_ref, wout_ref, gffn_ref,
                      wup_ref, wdown_ref, gfin_ref, o_ref, *, final, hidden_chunk):
    d = x_ref.shape[1]
    pa = _dot(ya_ref[...], wba_ref[...])
    pb = _dot(yb_ref[...], wbb_ref[...])
    merged = _sigmoid(zg_ref[:, 0:d]) * pa + _sigmoid(zg_ref[:, d:2 * d]) * pb
    x1 = x_ref[...] + _dot(merged.astype(BF16), wout_ref[...])
    h2 = _rms(x1, gffn_ref[...]).astype(BF16)
    x2 = x1
    hidden = wup_ref.shape[1]
    for c in range(hidden // hidden_chunk):
        sl = slice(c * hidden_chunk, (c + 1) * hidden_chunk)
        u = jnp.maximum(_dot(h2, wup_ref[:, sl]), 0.0)
        x2 = x2 + _dot((u * u).astype(BF16), wdown_ref[sl, :])
    o_ref[...] = _rms(x2, gfin_ref[...]) if final else x2


def _merge_ffn(x2, ya, yb, zg, wba, wbb, wout, gffn, wup, wdown, gfin, tm, final):
    n, d = x2.shape
    const = lambda i: (0, 0)
    row = lambda i: (i, 0)
    resident = lambda arr: pl.BlockSpec(arr.shape, const, pipeline_mode=pl.Buffered(1))
    kernel = functools.partial(_merge_ffn_kernel, final=final,
                               hidden_chunk=min(1024, wup.shape[1]))
    return pl.pallas_call(
        kernel,
        grid=(n // tm,),
        in_specs=[
            pl.BlockSpec((tm, d), row),
            pl.BlockSpec((tm, WIDTH), row),
            pl.BlockSpec((tm, WIDTH), row),
            pl.BlockSpec((tm, 2 * d), row),
            resident(wba), resident(wbb), resident(wout),
            pl.BlockSpec((1, d), const),
            resident(wup), resident(wdown),
            pl.BlockSpec((1, d), const),
        ],
        out_specs=pl.BlockSpec((tm, d), row),
        out_shape=jax.ShapeDtypeStruct((n, d), F32),
        compiler_params=pltpu.CompilerParams(
            dimension_semantics=("arbitrary",), vmem_limit_bytes=VMEM_LIMIT),
        name="merge_ffn",
    )(x2, ya, yb, zg, wba, wbb, wout, gffn, wup, wdown, gfin)


def _rope_tables(seq):
    inv = 1.0 / (ROPE_THETA ** (jnp.arange(0, HEAD_DIM, 2, dtype=F32) / HEAD_DIM))
    ang = jnp.arange(seq, dtype=F32)[:, None] * inv[None, :]
    cos, sin = jnp.cos(ang), jnp.sin(ang)
    cos_t = jnp.concatenate([cos, cos] * (LANES // HEAD_DIM), axis=1)
    sin_t = jnp.concatenate([-sin, sin] * (LANES // HEAD_DIM), axis=1)
    return cos_t, sin_t


def _pad_cols(a, n):
    return jnp.pad(a, ((0, 0), (0, n - a.shape[1])))


def _pad_rows(a, n):
    return jnp.pad(a, ((0, n - a.shape[0]), (0, 0)))


def kernel(x, g_mix, w_in, mu_shift, decay_bias, w_decay_up, iclr_bias, w_iclr_up, w_gate_up,
           k_k, k_a, r_k, gn_w, gn_b, w_branch, w_out, g_ffn, w_ffn_up, w_ffn_down, g_final):
    batch, seq, d = x.shape
    depth = g_mix.shape[0]
    n = batch * seq
    tm = 256
    assert seq % Q_BLOCK == 0 and seq % min(KEY_CHUNK, seq) == 0
    assert seq % min(RWKV_BLOCK, seq) == 0 and min(RWKV_BLOCK, seq) % RWKV_CHUNK == 0
    assert n % tm == 0
    cos_t, sin_t = _rope_tables(seq)
    row = lambda a: a.reshape(1, -1)
    xf = x.reshape(n, d)
    for l in range(depth):
        w = w_in[l]
        w_rwkv = _pad_cols(w[:, :RWKV_IN], RWKV_IN_PAD).astype(BF16)
        wd = w[:, RWKV_IN:RWKV_IN + DSA_IN]
        o_k, o_v, o_qi = WIDTH, WIDTH + HEAD_DIM, WIDTH + 2 * HEAD_DIM
        o_ki, o_wi = o_qi + WIDTH, o_qi + WIDTH + HEAD_DIM
        w_dsa = _pad_cols(jnp.concatenate(
            [wd[:, :WIDTH], wd[:, o_qi:o_ki], wd[:, o_k:o_v], wd[:, o_v:o_qi],
             wd[:, o_ki:o_wi], wd[:, o_wi:]], axis=1), DSA_IN_PAD).astype(BF16)
        w_gate = w[:, RWKV_IN + DSA_IN:].astype(BF16)
        zr, zd, zg = _inproj(xf, row(g_mix[l]), w_rwkv, w_dsa, w_gate, tm)

        wdec = _pad_rows(w_decay_up[l], LANES).astype(BF16)
        wicl = jnp.pad(w_iclr_up[l], ((DECAY_LORA, 0), (0, 0))).astype(BF16)
        wgate = _pad_rows(w_gate_up[l], GATE_LORA_PAD).astype(BF16)
        mu = _pad_cols(row(mu_shift[l]), RWKV_IN_PAD)
        ya = _rwkv(zr, batch, seq, mu, row(decay_bias[l]), wdec, row(iclr_bias[l]), wicl,
                   wgate, row(k_k[l]), row(k_a[l]), row(r_k[l]), row(gn_w[l]), row(gn_b[l]))
        yb = _dsa(zd, batch, seq, cos_t, sin_t)

        xf = _merge_ffn(xf, ya, yb, zg, w_branch[l, 0].astype(BF16),
                        w_branch[l, 1].astype(BF16), w_out[l].astype(BF16), row(g_ffn[l]),
                        w_ffn_up[l].astype(BF16), w_ffn_down[l].astype(BF16), row(g_final),
                        tm, final=(l == depth - 1))
    return xf.reshape(batch, seq, d)
```

```python
import functools

import numpy as np
import jax
import jax.numpy as jnp
from jax import lax
from jax.experimental import pallas as pl
from jax.experimental.pallas import tpu as pltpu

F32 = jnp.float32
BF16 = jnp.bfloat16

HEADS = 8
HEAD_DIM = 64
PAIRS = HEADS // 2
WIDTH = HEADS * HEAD_DIM
DECAY_LORA = 64
ICLR_LORA = 64
GATE_LORA = 160
GATE_LORA_PAD = 256
RWKV_IN = 3 * WIDTH + DECAY_LORA + ICLR_LORA + GATE_LORA
RWKV_IN_PAD = 3 * WIDTH + DECAY_LORA + ICLR_LORA + GATE_LORA_PAD
DSA_IN = WIDTH + 2 * HEAD_DIM + WIDTH + HEAD_DIM + HEADS
DSA_IN_PAD = 2 * WIDTH + 2 * 128
TOPK_MAX = 256
Q_TILE = 256
ROPE_THETA = 10000.0
NORM_EPS = 1e-6
RWKV_GN_EPS = 64e-5
L2_EPS = 1e-12

LANES = 128
SUBLANES = 8
VMEM_LIMIT = 56 * 1024 * 1024

RWKV_CHUNK = 64
RWKV_BLOCK = 256
RWKV_CHUNKS_PER_TRIP = 4
KEY_CHUNK = 256
SEARCH_UNROLL = 4
SEARCH_ITERS_MAX = 64
MASK_BIAS = -1e30


def _dot(a, b):
    return jnp.dot(a, b, preferred_element_type=F32)


def _dot_nt(a, b):
    return lax.dot_general(a, b, (((1,), (1,)), ((), ())), preferred_element_type=F32)


def _dot_tn(a, b):
    return lax.dot_general(a, b, (((0,), (0,)), ((), ())), preferred_element_type=F32)


def _split2(x):
    hi = x.astype(BF16)
    lo = (x - hi.astype(F32)).astype(BF16)
    return hi, lo


def _split3(x):
    h1 = x.astype(BF16)
    r1 = x - h1.astype(F32)
    h2 = r1.astype(BF16)
    h3 = (r1 - h2.astype(F32)).astype(BF16)
    return h1, h2, h3


def _rms(x, g):
    ms = jnp.mean(x * x, axis=-1, keepdims=True)
    return x * lax.rsqrt(ms + NORM_EPS) * g


def _sigmoid(x):
    return 1.0 / (1.0 + jnp.exp(-x))


def _swap_halves(x):
    n = x.shape[1]
    lane = lax.broadcasted_iota(jnp.int32, x.shape, 1)
    first = (lane % LANES) < HEAD_DIM
    return jnp.where(first, pltpu.roll(x, n - HEAD_DIM, 1), pltpu.roll(x, HEAD_DIM, 1))


def _inproj_kernel(x_ref, g_ref, wr_ref, wd_ref, wg_ref, zr_ref, zd_ref, zg_ref):
    h = _rms(x_ref[...], g_ref[...]).astype(BF16)
    zr_ref[...] = _dot(h, wr_ref[...])
    zd_ref[...] = _dot(h, wd_ref[...])
    zg_ref[...] = _dot(h, wg_ref[...])


def _inproj(x2, g, wr, wd, wg, tm):
    n, d = x2.shape
    const = lambda i: (0, 0)
    row = lambda i: (i, 0)
    return pl.pallas_call(
        _inproj_kernel,
        grid=(n // tm,),
        in_specs=[
            pl.BlockSpec((tm, d), row),
            pl.BlockSpec((1, d), const),
            pl.BlockSpec(wr.shape, const),
            pl.BlockSpec(wd.shape, const),
            pl.BlockSpec(wg.shape, const),
        ],
        out_specs=[
            pl.BlockSpec((tm, wr.shape[1]), row),
            pl.BlockSpec((tm, wd.shape[1]), row),
            pl.BlockSpec((tm, wg.shape[1]), row),
        ],
        out_shape=[
            jax.ShapeDtypeStruct((n, wr.shape[1]), F32),
            jax.ShapeDtypeStruct((n, wd.shape[1]), F32),
            jax.ShapeDtypeStruct((n, wg.shape[1]), F32),
        ],
        compiler_params=pltpu.CompilerParams(
            dimension_semantics=("arbitrary",), vmem_limit_bytes=VMEM_LIMIT),
        name="inproj",
    )(x2, g, wr, wd, wg)


def _rwkv_kernel(z_ref, mu_ref, dbias_ref, wdec_ref, ibias_ref, wicl_ref, wgate_ref,
                 kk_ref, ka_ref, rk_ref, gnw_ref, gnb_ref, tri_ref, blk_ref, bd_ref,
                 y_ref,
                 state_ref, carry_ref, qa_ref, bk_ref, bkh_ref, vsw_ref, pc_ref, yacc_ref):
    tb = z_ref.shape[0]
    c_len = RWKV_CHUNK
    n_chunks = tb // c_len

    @pl.when(pl.program_id(1) == 0)
    def _():
        state_ref[...] = jnp.zeros_like(state_ref)
        carry_ref[...] = jnp.zeros_like(carry_ref)

    z = z_ref[...]
    prev = pltpu.roll(z, 1, 0)
    first_row = lax.broadcasted_iota(jnp.int32, (tb, 1), 0) == 0
    prev = jnp.where(first_row, carry_ref[0:1, :], prev)
    carry_ref[0:1, :] = z[tb - 1:tb, :]
    zs = z + (prev - z) * mu_ref[...]

    r = zs[:, 0:WIDTH]
    k = zs[:, WIDTH:2 * WIDTH]
    v = zs[:, 2 * WIDTH:3 * WIDTH]
    lora_in = zs[:, 3 * WIDTH:3 * WIDTH + LANES]
    gate_in = zs[:, 3 * WIDTH + LANES:]

    dec_pre = dbias_ref[...] + _dot(jnp.tanh(lora_in).astype(BF16), wdec_ref[...])
    neg = -dec_pre
    softplus = jnp.maximum(neg, 0.0) + jnp.log1p(jnp.exp(-jnp.abs(neg)))
    w_log = -softplus - 0.5
    logw = -jnp.exp(w_log)
    iclr = _sigmoid(ibias_ref[...] + _dot(lora_in.astype(BF16), wicl_ref[...]))
    gate = _dot(_sigmoid(gate_in).astype(BF16), wgate_ref[...])

    bd = bd_ref[...]

    def headsum(t):
        hi, lo = _split2(t)
        return _dot(hi, bd) + _dot(lo, bd)

    def chunk_sum(mat, t):
        hi, lo = _split2(t)
        return _dot(mat, hi) + _dot(mat, lo)

    kk = k * kk_ref[...]
    kk = kk / jnp.maximum(jnp.sqrt(headsum(kk * kk)), L2_EPS)
    k2 = k * (1.0 + (iclr - 1.0) * ka_ref[...])
    a = -kk
    b = kk * iclr

    cum = chunk_sum(tri_ref[...], logw)
    cum_end = chunk_sum(blk_ref[...], logw)
    e_neg = jnp.exp(-cum)
    e_tail = jnp.exp(cum_end - cum)
    at = (a * jnp.exp(cum - logw)).astype(BF16)
    rt = (r * jnp.exp(cum)).astype(BF16)
    bt = (b * e_neg).astype(BF16)
    kt = (k2 * e_neg).astype(BF16)
    bh = (b * e_tail).astype(BF16)
    kh = (k2 * e_tail).astype(BF16)
    for c in range(n_chunks):
        src = slice(c * c_len, (c + 1) * c_len)
        top = slice(2 * c * c_len, (2 * c + 1) * c_len)
        bot = slice((2 * c + 1) * c_len, (2 * c + 2) * c_len)
        qa_ref[top, :] = at[src]
        qa_ref[bot, :] = rt[src]
        bk_ref[top, :] = bt[src]
        bk_ref[bot, :] = kt[src]
        bkh_ref[top, :] = bh[src]
        bkh_ref[bot, :] = kh[src]
    vsw_ref[...] = _swap_halves(v).astype(BF16)
    pc_ref[...] = jnp.exp(cum_end)

    ri = lax.broadcasted_iota(jnp.int32, (c_len, 2 * c_len), 0)
    ci = lax.broadcasted_iota(jnp.int32, (c_len, 2 * c_len), 1) % c_len
    strict = ci < ri
    incl = ci <= ri
    lane = lax.broadcasted_iota(jnp.int32, (1, LANES), 1)
    half = [lane < HEAD_DIM, lane >= HEAD_DIM]
    zero_bf = jnp.zeros((), BF16)
    zero_top = jnp.zeros((c_len, LANES), BF16)
    hs = range(HEADS)
    per_trip = min(RWKV_CHUNKS_PER_TRIP, n_chunks)
    items = [(i, h) for i in range(per_trip) for h in hs]
    lanes = [slice((h // 2) * LANES, (h // 2 + 1) * LANES) for h in hs]
    own = [half[h % 2] for h in hs]
    other = [half[1 - h % 2] for h in hs]

    def trip_body(t, carry):
        cs = [t * per_trip + i for i in range(per_trip)]
        rows2 = [pl.ds(pl.multiple_of(c * 2 * c_len, 2 * c_len), 2 * c_len) for c in cs]
        rows = [pl.ds(pl.multiple_of(c * c_len, c_len), c_len) for c in cs]

        qa = {(i, h): jnp.where(own[h], qa_ref[rows2[i], lanes[h]], zero_bf) for i, h in items}
        ab = {(i, h): _dot_nt(qa[i, h], bk_ref[rows2[i], lanes[h]]) for i, h in items}
        mtop = {k: jnp.where(strict, ab[k][:c_len], 0.0) for k in items}
        mbot = {k: jnp.where(incl, ab[k][c_len:], 0.0).astype(BF16) for k in items}
        vp = {(i, h): jnp.where(other[h], vsw_ref[rows[i], lanes[h]], zero_bf) for i, h in items}
        lkv = {k: _dot(mtop[k].astype(BF16), jnp.concatenate([zero_top, vp[k]], axis=0))
               for k in items}
        x = {k: qa[k][:c_len].astype(F32) + lkv[k] for k in items}
        pw = {k: mtop[k][:, :c_len].astype(BF16) for k in items}
        steps = int(np.log2(c_len))
        for j in range(steps):
            x = {k: x[k] + _dot(pw[k], x[k].astype(BF16)) for k in items}
            if j + 1 < steps:
                pw = {k: _dot(pw[k], pw[k]).astype(BF16) for k in items}
        rhs = {k: jnp.concatenate([x[k].astype(BF16), vp[k]], axis=0) for k in items}
        gy = {k: _dot(mbot[k], rhs[k]) for k in items}
        mn = {(i, h): _dot_tn(rhs[i, h],
                              jnp.where(own[h], bkh_ref[rows2[i], lanes[h]], zero_bf))
              for i, h in items}
        gr = {k: qa[k][c_len:].astype(F32) + gy[k] for k in items}

        st = [state_ref[p] for p in range(PAIRS)]
        for i in range(per_trip):
            ss = [jnp.concatenate([st[p].astype(BF16)] * 2, axis=0) for p in range(PAIRS)]
            gm = [jnp.where(own[h], gr[i, h], 0.0).astype(BF16) for h in hs]
            yfull = [gy[i, h] + _dot_nt(gm[h], ss[h // 2]) for h in hs]
            sm = [jnp.where(own[h], st[h // 2], 0.0).astype(BF16) for h in hs]
            upd = [_dot(sm[h], mn[i, h].astype(BF16))
                   + mn[i, h][(1 - h % 2) * c_len:(2 - h % 2) * c_len] for h in hs]
            for p in range(PAIRS):
                pl_lanes = slice(p * LANES, (p + 1) * LANES)
                pc = pc_ref[pl.ds(pl.multiple_of(cs[i] * c_len, c_len), 1), pl_lanes]
                st[p] = st[p] * pc + upd[2 * p] + upd[2 * p + 1]
                yacc_ref[rows[i], pl_lanes] = jnp.where(half[0], yfull[2 * p + 1], yfull[2 * p])
        for p in range(PAIRS):
            state_ref[p] = st[p]
        return carry

    lax.fori_loop(0, n_chunks // per_trip, trip_body, 0)

    y = _swap_halves(yacc_ref[...])
    inv_n = 1.0 / HEAD_DIM
    mean = headsum(y) * inv_n
    yc = y - mean
    var = headsum(yc * yc) * inv_n
    yn = yc * lax.rsqrt(var + RWKV_GN_EPS) * gnw_ref[...] + gnb_ref[...]
    bonus = headsum(r * k2 * rk_ref[...]) * v
    y_ref[...] = ((yn + bonus) * gate).astype(y_ref.dtype)


def _rwkv(zr, batch, seq, mu, dbias, wdec, ibias, wicl, wgate, kk, ka, rk, gnw, gnb):
    tb = min(RWKV_BLOCK, seq)
    n_t = seq // tb
    idx = np.arange(tb)
    same_chunk = (idx[:, None] // RWKV_CHUNK) == (idx[None, :] // RWKV_CHUNK)
    tri = jnp.asarray(same_chunk & (idx[None, :] <= idx[:, None]), BF16)
    blk = jnp.asarray(same_chunk, BF16)
    hid = np.arange(WIDTH) // HEAD_DIM
    bd = jnp.asarray(hid[:, None] == hid[None, :], BF16)
    const = lambda b, i: (0, 0)
    row = lambda b, i: (b * n_t + i, 0)
    vec = lambda n: pl.BlockSpec((1, n), const)
    full = lambda arr: pl.BlockSpec(arr.shape, const)
    return pl.pallas_call(
        _rwkv_kernel,
        grid=(batch, n_t),
        in_specs=[
            pl.BlockSpec((tb, RWKV_IN_PAD), row),
            vec(RWKV_IN_PAD), vec(WIDTH), full(wdec), vec(WIDTH), full(wicl), full(wgate),
            vec(WIDTH), vec(WIDTH), vec(WIDTH), vec(WIDTH), vec(WIDTH),
            full(tri), full(blk), full(bd),
        ],
        out_specs=pl.BlockSpec((tb, WIDTH), row),
        out_shape=jax.ShapeDtypeStruct((batch * seq, WIDTH), BF16),
        scratch_shapes=[
            pltpu.VMEM((PAIRS, HEAD_DIM, LANES), F32),
            pltpu.VMEM((SUBLANES, RWKV_IN_PAD), F32),
            pltpu.VMEM((2 * tb, WIDTH), BF16),
            pltpu.VMEM((2 * tb, WIDTH), BF16),
            pltpu.VMEM((2 * tb, WIDTH), BF16),
            pltpu.VMEM((tb, WIDTH), BF16),
            pltpu.VMEM((tb, WIDTH), F32),
            pltpu.VMEM((tb, WIDTH), F32),
        ],
        compiler_params=pltpu.CompilerParams(
            dimension_semantics=("arbitrary", "arbitrary"), vmem_limit_bytes=VMEM_LIMIT),
        name="rwkv7",
    )(zr, mu, dbias, wdec, ibias, wicl, wgate, kk, ka, rk, gnw, gnb, tri, blk, bd)


def _rope(x, cos, sin_signed):
    n = x.shape[1] // LANES
    lane = lax.broadcasted_iota(jnp.int32, x.shape, 1)
    first_half = (lane % HEAD_DIM) < (HEAD_DIM // 2)
    swapped = jnp.where(first_half,
                        pltpu.roll(x, x.shape[1] - HEAD_DIM // 2, 1),
                        pltpu.roll(x, HEAD_DIM // 2, 1))
    if n > 1:
        cos = jnp.concatenate([cos] * n, axis=1)
        sin_signed = jnp.concatenate([sin_signed] * n, axis=1)
    return x * cos + swapped * sin_signed


def _fold(x, fn):
    rows, n = x.shape
    parts = x.reshape(rows // SUBLANES, SUBLANES, n)
    level = [parts[i] for i in range(rows // SUBLANES)]
    while len(level) > 1:
        nxt = [fn(level[i], level[i + 1]) for i in range(0, len(level) - 1, 2)]
        if len(level) % 2:
            nxt.append(level[-1])
        level = nxt
    return level[0]


def _heads_to_rows(x):
    return jnp.concatenate(
        [x[:, h * HEAD_DIM:(h + 1) * HEAD_DIM] for h in range(HEADS)], axis=0)


def _dsa_kernel(qq_ref, kv_ref, kiw_ref, wq_ref, cosq_ref, sinq_ref, cosk_ref, sink_ref,
                o_ref,
                k_s, ki_s, vt_s, sc_s, s_all, acc_s, *, topk, idx_scale):
    j = pl.program_id(1)
    tq = qq_ref.shape[0]
    seq = kv_ref.shape[0]
    kc = min(KEY_CHUNK, seq)
    n_ch = (j * tq + tq + kc - 1) // kc
    inf = jnp.float32(jnp.inf)

    @pl.when(j == 0)
    def _():
        kv = kv_ref[...]
        kvr = _rope(kv, cosk_ref[...], sink_ref[...])
        k_s[...] = kvr[:, :HEAD_DIM].astype(BF16)
        kvt = kv.T
        for c in range(seq // kc):
            vt_s[c] = kvt[HEAD_DIM:, c * kc:(c + 1) * kc].astype(BF16)
        kiw = kiw_ref[...]
        ki_s[...] = _rope(kiw, cosk_ref[...], sink_ref[...])[:, :HEAD_DIM].astype(BF16)

    cosq = cosq_ref[...]
    sinq = sinq_ref[...]
    q_all = _heads_to_rows(_rope(qq_ref[:, 0:WIDTH], cosq, sinq)
                           * (HEAD_DIM ** -0.5)).astype(BF16)
    qi_all = _heads_to_rows(_rope(qq_ref[:, WIDTH:2 * WIDTH], cosq, sinq)).astype(BF16)

    sel_r = lax.broadcasted_iota(jnp.int32, (2 * SUBLANES, LANES), 0)
    sel_c = lax.broadcasted_iota(jnp.int32, (2 * SUBLANES, LANES), 1)
    sel = (sel_c == sel_r + HEAD_DIM).astype(BF16)
    w1, w2, w3 = _split3(wq_ref[...])
    w_t = (_dot_nt(sel, w1) + (_dot_nt(sel, w2) + _dot_nt(sel, w3))) * idx_scale

    qpos = j * tq + lax.broadcasted_iota(jnp.int32, (1, tq), 1)
    key_in_chunk = lax.broadcasted_iota(jnp.int32, (kc, tq), 0)

    def chunk_rows(c):
        return pl.ds(pl.multiple_of(c * kc, kc), kc)

    def score_body(c, carry):
        rmax, rmin = carry
        ki = ki_s[chunk_rows(c), :]
        lg = [_dot_nt(ki, qi_all[h * tq:(h + 1) * tq]) for h in range(HEADS)]
        acc = w_t[0:1] * jnp.maximum(lg[0], 0.0)
        for h in range(1, HEADS):
            acc = acc + w_t[h:h + 1] * jnp.maximum(lg[h], 0.0)
        causal = (c * kc + key_in_chunk) <= qpos
        lo_fill = jnp.where(causal, acc, -inf)
        sc_s[chunk_rows(c), :] = lo_fill
        rmax = jnp.maximum(rmax, _fold(lo_fill, jnp.maximum))
        rmin = jnp.minimum(rmin, _fold(jnp.where(causal, acc, inf), jnp.minimum))
        return rmax, rmin

    rmax, rmin = lax.fori_loop(
        0, n_ch, score_body,
        (jnp.full((SUBLANES, tq), -inf, F32), jnp.full((SUBLANES, tq), inf, F32)))
    rmax = jnp.max(rmax, axis=0, keepdims=True)
    rmin = jnp.min(rmin, axis=0, keepdims=True)

    def count(pred_fn):
        def body(c, acc):
            return acc + _fold(pred_fn(sc_s[chunk_rows(c), :]).astype(F32), jnp.add)
        acc = lax.fori_loop(0, n_ch, body, jnp.zeros((SUBLANES, tq), F32))
        return jnp.sum(acc, axis=0, keepdims=True)

    kf = jnp.float32(topk)
    n_causal = (qpos + 1).astype(F32)
    all_keys = n_causal < kf
    c_max = count(lambda s: s >= rmax)
    top_tied = c_max >= kf
    lo = jnp.where(top_tied, rmax, rmin)
    c_lo = jnp.where(top_tied, c_max, n_causal)
    hi = rmax
    done = all_keys | (c_lo == kf) | (lo == hi)

    def search_cond(st):
        _, _, _, done_f, it = st
        return jnp.logical_and(jnp.min(done_f) < 0.5, it < SEARCH_ITERS_MAX)

    def search_body(st):
        lo, hi, c_lo, done_f, it = st
        for _ in range(SEARCH_UNROLL):
            active = done_f < 0.5
            mid = lo * 0.5 + hi * 0.5
            c = count(lambda s: s >= mid)
            ge = c >= kf
            stuck = (mid == lo) | (mid == hi)
            lo = jnp.where(active & ge, mid, lo)
            hi = jnp.where(active & jnp.logical_not(ge), mid, hi)
            c_lo = jnp.where(active & ge, c, c_lo)
            done_f = jnp.where(active & ((c_lo == kf) | stuck), jnp.float32(1.0), done_f)
        return lo, hi, c_lo, done_f, it + SEARCH_UNROLL

    lo, hi, c_lo, _, _ = lax.while_loop(
        search_cond, search_body, (lo, hi, c_lo, done.astype(F32), jnp.int32(0)))
    lo = jnp.where(all_keys, -inf, lo)
    exact_k = all_keys | (c_lo == kf)
    no_ties = jnp.min(exact_k.astype(F32)) > 0.5

    @pl.when(no_ties)
    def _():
        def body(c, carry):
            s = sc_s[chunk_rows(c), :]
            causal = (c * kc + key_in_chunk) <= qpos
            sc_s[chunk_rows(c), :] = jnp.where((s >= lo) & causal, 0.0, MASK_BIAS)
            return carry
        lax.fori_loop(0, n_ch, body, 0)

    @pl.when(jnp.logical_not(no_ties))
    def _():
        def snap_body(c, acc):
            s = sc_s[chunk_rows(c), :]
            return jnp.minimum(acc, _fold(jnp.where(s >= lo, s, inf), jnp.minimum))
        thr = jnp.min(lax.fori_loop(0, n_ch, snap_body, jnp.full((SUBLANES, tq), inf, F32)),
                      axis=0, keepdims=True)
        need = kf - count(lambda s: s > thr)
        ki_ = lax.broadcasted_iota(jnp.int32, (kc, kc), 0)
        kj_ = lax.broadcasted_iota(jnp.int32, (kc, kc), 1)
        earlier = (kj_ < ki_).astype(BF16)
        ones = jnp.ones((2 * SUBLANES, kc), BF16)

        def body(c, run):
            s = sc_s[chunk_rows(c), :]
            causal = (c * kc + key_in_chunk) <= qpos
            eq = (s == thr).astype(BF16)
            rank = run + _dot(earlier, eq)
            take = (s > thr) | ((s == thr) & (rank < need))
            sc_s[chunk_rows(c), :] = jnp.where(take & causal, 0.0, MASK_BIAS)
            return run + _dot(ones, eq)[0:1]
        lax.fori_loop(0, n_ch, body, jnp.zeros((1, tq), F32))

    def logits_body(c, m_acc):
        kch = k_s[chunk_rows(c), :]
        bias = sc_s[chunk_rows(c), :]
        s = [_dot_nt(kch, q_all[h * tq:(h + 1) * tq]) for h in range(HEADS)]
        out = []
        for h in range(HEADS):
            sh = s[h] + bias
            s_all[chunk_rows(c), h * tq:(h + 1) * tq] = sh
            out.append(_fold(sh, jnp.maximum))
        return jnp.maximum(m_acc, jnp.concatenate(out, axis=1))

    m = jnp.max(lax.fori_loop(0, n_ch, logits_body,
                              jnp.full((SUBLANES, HEADS * tq), MASK_BIAS, F32)),
                axis=0, keepdims=True)

    def pv_body(c, l_acc):
        vt = vt_s[c]
        out = []
        for h in range(HEADS):
            cols = slice(h * tq, (h + 1) * tq)
            p = jnp.exp(s_all[chunk_rows(c), cols] - m[:, cols])
            out.append(_fold(p, jnp.add))
            acc_s[:, cols] += _dot(vt, p.astype(BF16))
        return l_acc + jnp.concatenate(out, axis=1)

    acc_s[...] = jnp.zeros_like(acc_s)
    l_acc = lax.fori_loop(0, n_ch, pv_body, jnp.zeros((SUBLANES, HEADS * tq), F32))
    out_t = acc_s[...] / jnp.sum(l_acc, axis=0, keepdims=True)
    for p in range(PAIRS):
        pair = jnp.concatenate([out_t[:, 2 * p * tq:(2 * p + 1) * tq],
                                out_t[:, (2 * p + 1) * tq:(2 * p + 2) * tq]], axis=0)
        o_ref[:, p * LANES:(p + 1) * LANES] = pair.T.astype(o_ref.dtype)


def _dsa(zd, batch, seq, cos_t, sin_t):
    tq = min(Q_TILE, seq)
    nb = seq // tq
    topk = min(TOPK_MAX, seq // 4)
    kc = min(KEY_CHUNK, seq)
    idx_scale = HEADS ** -0.5 * HEAD_DIM ** -0.5
    kernel = functools.partial(_dsa_kernel, topk=topk, idx_scale=idx_scale)
    qcols = 2 * WIDTH // LANES
    return pl.pallas_call(
        kernel,
        grid=(batch, nb),
        in_specs=[
            pl.BlockSpec((tq, 2 * WIDTH), lambda b, j: (b * nb + j, 0)),
            pl.BlockSpec((seq, LANES), lambda b, j: (b, qcols)),
            pl.BlockSpec((seq, LANES), lambda b, j: (b, qcols + 1)),
            pl.BlockSpec((tq, LANES), lambda b, j: (b * nb + j, qcols + 1)),
            pl.BlockSpec((tq, LANES), lambda b, j: (j, 0)),
            pl.BlockSpec((tq, LANES), lambda b, j: (j, 0)),
            pl.BlockSpec((seq, LANES), lambda b, j: (0, 0)),
            pl.BlockSpec((seq, LANES), lambda b, j: (0, 0)),
        ],
        out_specs=pl.BlockSpec((tq, WIDTH), lambda b, j: (b * nb + j, 0)),
        out_shape=jax.ShapeDtypeStruct((batch * seq, WIDTH), BF16),
        scratch_shapes=[
            pltpu.VMEM((seq, HEAD_DIM), BF16),
            pltpu.VMEM((seq, HEAD_DIM), BF16),
            pltpu.VMEM((seq // kc, HEAD_DIM, kc), BF16),
            pltpu.VMEM((seq, tq), F32),
            pltpu.VMEM((seq, HEADS * tq), F32),
            pltpu.VMEM((HEAD_DIM, HEADS * tq), F32),
        ],
        compiler_params=pltpu.CompilerParams(
            dimension_semantics=("arbitrary", "arbitrary"), vmem_limit_bytes=VMEM_LIMIT),
        name="dsa",
    )(zd, zd, zd, zd, cos_t, sin_t, cos_t, sin_t)


def _merge_ffn_kernel(x_ref, ya_ref, yb_ref, zg_ref, wba_ref, wbb_ref, wout_ref, gffn_ref,
                      wup_ref, wdown_ref, gfin_ref, o_ref, *, final, hidden_chunk):
    d = x_ref.shape[1]
    pa = _dot(ya_ref[...], wba_ref[...])
    pb = _dot(yb_ref[...], wbb_ref[...])
    merged = _sigmoid(zg_ref[:, 0:d]) * pa + _sigmoid(zg_ref[:, d:2 * d]) * pb
    x1 = x_ref[...] + _dot(merged.astype(BF16), wout_ref[...])
    h2 = _rms(x1, gffn_ref[...]).astype(BF16)
    x2 = x1
    hidden = wup_ref.shape[1]
    for c in range(hidden // hidden_chunk):
        sl = slice(c * hidden_chunk, (c + 1) * hidden_chunk)
        u = jnp.maximum(_dot(h2, wup_ref[:, sl]), 0.0)
        x2 = x2 + _dot((u * u).astype(BF16), wdown_ref[sl, :])
    o_ref[...] = _rms(x2, gfin_ref[...]) if final else x2


def _merge_ffn(x2, ya, yb, zg, wba, wbb, wout, gffn, wup, wdown, gfin, tm, final):
    n, d = x2.shape
    const = lambda i: (0, 0)
    row = lambda i: (i, 0)
    resident = lambda arr: pl.BlockSpec(arr.shape, const, pipeline_mode=pl.Buffered(1))
    kernel = functools.partial(_merge_ffn_kernel, final=final,
                               hidden_chunk=min(1024, wup.shape[1]))
    return pl.pallas_call(
        kernel,
        grid=(n // tm,),
        in_specs=[
            pl.BlockSpec((tm, d), row),
            pl.BlockSpec((tm, WIDTH), row),
            pl.BlockSpec((tm, WIDTH), row),
            pl.BlockSpec((tm, 2 * d), row),
            resident(wba), resident(wbb), resident(wout),
            pl.BlockSpec((1, d), const),
            resident(wup), resident(wdown),
            pl.BlockSpec((1, d), const),
        ],
        out_specs=pl.BlockSpec((tm, d), row),
        out_shape=jax.ShapeDtypeStruct((n, d), F32),
        compiler_params=pltpu.CompilerParams(
            dimension_semantics=("arbitrary",), vmem_limit_bytes=VMEM_LIMIT),
        name="merge_ffn",
    )(x2, ya, yb, zg, wba, wbb, wout, gffn, wup, wdown, gfin)


def _rope_tables(seq):
    inv = 1.0 / (ROPE_THETA ** (jnp.arange(0, HEAD_DIM, 2, dtype=F32) / HEAD_DIM))
    ang = jnp.arange(seq, dtype=F32)[:, None] * inv[None, :]
    cos, sin = jnp.cos(ang), jnp.sin(ang)
    cos_t = jnp.concatenate([cos, cos] * (LANES // HEAD_DIM), axis=1)
    sin_t = jnp.concatenate([-sin, sin] * (LANES // HEAD_DIM), axis=1)
    return cos_t, sin_t


def _pad_cols(a, n):
    return jnp.pad(a, ((0, 0), (0, n - a.shape[1])))


def _pad_rows(a, n):
    return jnp.pad(a, ((0, n - a.shape[0]), (0, 0)))


def kernel(x, g_mix, w_in, mu_shift, decay_bias, w_decay_up, iclr_bias, w_iclr_up, w_gate_up,
           k_k, k_a, r_k, gn_w, gn_b, w_branch, w_out, g_ffn, w_ffn_up, w_ffn_down, g_final):
    batch, seq, d = x.shape
    depth = g_mix.shape[0]
    n = batch * seq
    tm = 256
    assert seq % min(Q_TILE, seq) == 0 and seq % min(KEY_CHUNK, seq) == 0
    assert seq % min(RWKV_BLOCK, seq) == 0 and min(RWKV_BLOCK, seq) % RWKV_CHUNK == 0
    assert n % tm == 0
    cos_t, sin_t = _rope_tables(seq)
    row = lambda a: a.reshape(1, -1)
    xf = x.reshape(n, d)
    for l in range(depth):
        w = w_in[l]
        w_rwkv = _pad_cols(w[:, :RWKV_IN], RWKV_IN_PAD).astype(BF16)
        wd = w[:, RWKV_IN:RWKV_IN + DSA_IN]
        o_k, o_v, o_qi = WIDTH, WIDTH + HEAD_DIM, WIDTH + 2 * HEAD_DIM
        o_ki, o_wi = o_qi + WIDTH, o_qi + WIDTH + HEAD_DIM
        w_dsa = _pad_cols(jnp.concatenate(
            [wd[:, :WIDTH], wd[:, o_qi:o_ki], wd[:, o_k:o_v], wd[:, o_v:o_qi],
             wd[:, o_ki:o_wi], wd[:, o_wi:]], axis=1), DSA_IN_PAD).astype(BF16)
        w_gate = w[:, RWKV_IN + DSA_IN:].astype(BF16)
        zr, zd, zg = _inproj(xf, row(g_mix[l]), w_rwkv, w_dsa, w_gate, tm)

        wdec = _pad_rows(w_decay_up[l], LANES).astype(BF16)
        wicl = jnp.pad(w_iclr_up[l], ((DECAY_LORA, 0), (0, 0))).astype(BF16)
        wgate = _pad_rows(w_gate_up[l], GATE_LORA_PAD).astype(BF16)
        mu = _pad_cols(row(mu_shift[l]), RWKV_IN_PAD)
        ya = _rwkv(zr, batch, seq, mu, row(decay_bias[l]), wdec, row(iclr_bias[l]), wicl,
                   wgate, row(k_k[l]), row(k_a[l]), row(r_k[l]), row(gn_w[l]), row(gn_b[l]))
        yb = _dsa(zd, batch, seq, cos_t, sin_t)

        xf = _merge_ffn(xf, ya, yb, zg, w_branch[l, 0].astype(BF16),
                        w_branch[l, 1].astype(BF16), w_out[l].astype(BF16), row(g_ffn[l]),
                        w_ffn_up[l].astype(BF16), w_ffn_down[l].astype(BF16), row(g_final),
                        tm, final=(l == depth - 1))
    return xf.reshape(batch, seq, d)
```

```python
import functools

import numpy as np
import jax
import jax.numpy as jnp
from jax import lax
from jax.experimental import pallas as pl
from jax.experimental.pallas import tpu as pltpu

F32 = jnp.float32
BF16 = jnp.bfloat16

HEADS = 8
HEAD_DIM = 64
PAIRS = HEADS // 2
WIDTH = HEADS * HEAD_DIM
DECAY_LORA = 64
ICLR_LORA = 64
GATE_LORA = 160
GATE_LORA_PAD = 256
RWKV_IN = 3 * WIDTH + DECAY_LORA + ICLR_LORA + GATE_LORA
RWKV_IN_PAD = 3 * WIDTH + DECAY_LORA + ICLR_LORA + GATE_LORA_PAD
DSA_IN = WIDTH + 2 * HEAD_DIM + WIDTH + HEAD_DIM + HEADS
DSA_IN_PAD = 2 * WIDTH + 2 * 128
TOPK_MAX = 256
Q_TILE = 256
ROPE_THETA = 10000.0
NORM_EPS = 1e-6
RWKV_GN_EPS = 64e-5
L2_EPS = 1e-12

LANES = 128
SUBLANES = 8
MXU_TILE = 256
VMEM_LIMIT = 56 * 1024 * 1024

RWKV_CHUNK = 64
RWKV_BLOCK = 256
RWKV_CHUNKS_PER_TRIP = 4
KEY_CHUNK = 256
SEARCH_UNROLL = 4
SEARCH_ITERS_MAX = 320
COUNT_CHAINS = 4
MASK_BIAS = -1e30


def _dot(a, b):
    return jnp.dot(a, b, preferred_element_type=F32)


def _dot_nt(a, b):
    return lax.dot_general(a, b, (((1,), (1,)), ((), ())), preferred_element_type=F32)


def _dot_tn(a, b):
    return lax.dot_general(a, b, (((0,), (0,)), ((), ())), preferred_element_type=F32)


def _split2(x):
    hi = x.astype(BF16)
    lo = (x - hi.astype(F32)).astype(BF16)
    return hi, lo


def _split3(x):
    h1 = x.astype(BF16)
    r1 = x - h1.astype(F32)
    h2 = r1.astype(BF16)
    h3 = (r1 - h2.astype(F32)).astype(BF16)
    return h1, h2, h3


def _rms(x, g):
    ms = jnp.mean(x * x, axis=-1, keepdims=True)
    return x * lax.rsqrt(ms + NORM_EPS) * g


def _sigmoid(x):
    return 1.0 / (1.0 + jnp.exp(-x))


def _swap_halves(x):
    n = x.shape[1]
    lane = lax.broadcasted_iota(jnp.int32, x.shape, 1)
    first = (lane % LANES) < HEAD_DIM
    return jnp.where(first, pltpu.roll(x, n - HEAD_DIM, 1), pltpu.roll(x, HEAD_DIM, 1))


def _inproj_kernel(x_ref, g_ref, wr_ref, wd_ref, wg_ref, zr_ref, zd_ref, zg_ref):
    h = _rms(x_ref[...], g_ref[...]).astype(BF16)
    zr_ref[...] = _dot(h, wr_ref[...])
    zd_ref[...] = _dot(h, wd_ref[...])
    zg_ref[...] = _dot(h, wg_ref[...])


def _inproj(x2, g, wr, wd, wg, tm):
    n, d = x2.shape
    const = lambda i: (0, 0)
    row = lambda i: (i, 0)
    return pl.pallas_call(
        _inproj_kernel,
        grid=(n // tm,),
        in_specs=[
            pl.BlockSpec((tm, d), row),
            pl.BlockSpec((1, d), const),
            pl.BlockSpec(wr.shape, const),
            pl.BlockSpec(wd.shape, const),
            pl.BlockSpec(wg.shape, const),
        ],
        out_specs=[
            pl.BlockSpec((tm, wr.shape[1]), row),
            pl.BlockSpec((tm, wd.shape[1]), row),
            pl.BlockSpec((tm, wg.shape[1]), row),
        ],
        out_shape=[
            jax.ShapeDtypeStruct((n, wr.shape[1]), F32),
            jax.ShapeDtypeStruct((n, wd.shape[1]), F32),
            jax.ShapeDtypeStruct((n, wg.shape[1]), F32),
        ],
        compiler_params=pltpu.CompilerParams(
            dimension_semantics=("arbitrary",), vmem_limit_bytes=VMEM_LIMIT),
        name="inproj",
    )(x2, g, wr, wd, wg)


def _rwkv_kernel(z_ref, mu_ref, dbias_ref, wdec_ref, ibias_ref, wicl_ref, wgate_ref,
                 kk_ref, ka_ref, rk_ref, gnw_ref, gnb_ref, tri_ref, bd_ref,
                 y_ref,
                 state_ref, carry_ref, qa_ref, bk_ref, bkh_ref, vsw_ref, pc_ref, yacc_ref):
    tb = z_ref.shape[0]
    c_len = RWKV_CHUNK
    n_chunks = tb // c_len

    @pl.when(pl.program_id(1) == 0)
    def _():
        state_ref[...] = jnp.zeros_like(state_ref)
        carry_ref[...] = jnp.zeros_like(carry_ref)

    z = z_ref[...]
    prev = pltpu.roll(z, 1, 0)
    first_row = lax.broadcasted_iota(jnp.int32, (tb, 1), 0) == 0
    prev = jnp.where(first_row, carry_ref[0:1, :], prev)
    carry_ref[0:1, :] = z[tb - 1:tb, :]
    zs = z + (prev - z) * mu_ref[...]

    r = zs[:, 0:WIDTH]
    k = zs[:, WIDTH:2 * WIDTH]
    v = zs[:, 2 * WIDTH:3 * WIDTH]
    lora_in = zs[:, 3 * WIDTH:3 * WIDTH + LANES]
    gate_in = zs[:, 3 * WIDTH + LANES:]

    dec_pre = dbias_ref[...] + _dot(jnp.tanh(lora_in).astype(BF16), wdec_ref[...])
    neg = -dec_pre
    softplus = jnp.maximum(neg, 0.0) + jnp.log1p(jnp.exp(-jnp.abs(neg)))
    w_log = -softplus - 0.5
    logw = -jnp.exp(w_log)
    iclr = _sigmoid(ibias_ref[...] + _dot(lora_in.astype(BF16), wicl_ref[...]))
    gate = _dot(_sigmoid(gate_in).astype(BF16), wgate_ref[...])

    bd = bd_ref[...]
    bd_w = bd.shape[0]

    def headsum(t):
        tb16 = t.astype(BF16)
        return jnp.concatenate(
            [_dot(tb16[:, i:i + bd_w], bd) for i in range(0, WIDTH, bd_w)], axis=1)

    kk = k * kk_ref[...]
    kk = kk / jnp.maximum(jnp.sqrt(headsum(kk * kk)), L2_EPS)
    k2 = k * (1.0 + (iclr - 1.0) * ka_ref[...])
    a = -kk
    b = kk * iclr

    logw_hi, logw_lo = _split2(logw)
    tri = tri_ref[...]
    cum = _dot(tri, logw_hi) + _dot(tri, logw_lo)
    ends = [cum[(c + 1) * c_len - 1:(c + 1) * c_len, :] for c in range(n_chunks)]
    cum_end = jnp.concatenate([jnp.broadcast_to(e, (c_len, WIDTH)) for e in ends], axis=0)
    e_neg = jnp.exp(-cum)
    e_tail = jnp.exp(cum_end - cum)
    at = (a * jnp.exp(cum - logw)).astype(BF16)
    rt = (r * jnp.exp(cum)).astype(BF16)
    bt = (b * e_neg).astype(BF16)
    kt = (k2 * e_neg).astype(BF16)
    bh = (b * e_tail).astype(BF16)
    kh = (k2 * e_tail).astype(BF16)
    for c in range(n_chunks):
        src = slice(c * c_len, (c + 1) * c_len)
        top = slice(2 * c * c_len, (2 * c + 1) * c_len)
        bot = slice((2 * c + 1) * c_len, (2 * c + 2) * c_len)
        qa_ref[top, :] = at[src]
        qa_ref[bot, :] = rt[src]
        bk_ref[top, :] = bt[src]
        bk_ref[bot, :] = kt[src]
        bkh_ref[top, :] = bh[src]
        bkh_ref[bot, :] = kh[src]
        pc_ref[c * SUBLANES:c * SUBLANES + 1, :] = jnp.exp(ends[c])
    vsw_ref[...] = _swap_halves(v).astype(BF16)

    ri = lax.broadcasted_iota(jnp.int32, (c_len, 2 * c_len), 0)
    ci = lax.broadcasted_iota(jnp.int32, (c_len, 2 * c_len), 1) % c_len
    strict = ci < ri
    incl = ci <= ri
    lane = lax.broadcasted_iota(jnp.int32, (1, LANES), 1)
    half = [lane < HEAD_DIM, lane >= HEAD_DIM]
    zero_bf = jnp.zeros((), BF16)
    zero_top = jnp.zeros((c_len, LANES), BF16)
    hs = range(HEADS)
    per_trip = min(RWKV_CHUNKS_PER_TRIP, n_chunks)
    items = [(i, h) for i in range(per_trip) for h in hs]
    lanes = [slice((h // 2) * LANES, (h // 2 + 1) * LANES) for h in hs]
    own = [half[h % 2] for h in hs]
    other = [half[1 - h % 2] for h in hs]

    def trip_body(t, carry):
        cs = [t * per_trip + i for i in range(per_trip)]
        rows2 = [pl.ds(pl.multiple_of(c * 2 * c_len, 2 * c_len), 2 * c_len) for c in cs]
        rows = [pl.ds(pl.multiple_of(c * c_len, c_len), c_len) for c in cs]

        qa = {(i, h): jnp.where(own[h], qa_ref[rows2[i], lanes[h]], zero_bf) for i, h in items}
        ab = {(i, h): _dot_nt(qa[i, h], bk_ref[rows2[i], lanes[h]]) for i, h in items}
        mtop = {k: jnp.where(strict, ab[k][:c_len], 0.0) for k in items}
        mbot = {k: jnp.where(incl, ab[k][c_len:], 0.0).astype(BF16) for k in items}
        vp = {(i, h): jnp.where(other[h], vsw_ref[rows[i], lanes[h]], zero_bf) for i, h in items}
        lkv = {k: _dot(mtop[k].astype(BF16), jnp.concatenate([zero_top, vp[k]], axis=0))
               for k in items}
        x = {k: qa[k][:c_len].astype(F32) + lkv[k] for k in items}
        pw = {k: mtop[k][:, :c_len].astype(BF16) for k in items}
        steps = int(np.log2(c_len))
        for j in range(steps):
            x = {k: x[k] + _dot(pw[k], x[k].astype(BF16)) for k in items}
            if j + 1 < steps:
                pw = {k: _dot(pw[k], pw[k]).astype(BF16) for k in items}
        rhs = {k: jnp.concatenate([x[k].astype(BF16), vp[k]], axis=0) for k in items}
        gy = {k: _dot(mbot[k], rhs[k]) for k in items}
        mn = {(i, h): _dot_tn(rhs[i, h],
                              jnp.where(own[h], bkh_ref[rows2[i], lanes[h]], zero_bf))
              for i, h in items}
        gr = {k: qa[k][c_len:].astype(F32) + gy[k] for k in items}

        st = [state_ref[p] for p in range(PAIRS)]
        for i in range(per_trip):
            ss = [jnp.concatenate([st[p].astype(BF16)] * 2, axis=0) for p in range(PAIRS)]
            gm = [jnp.where(own[h], gr[i, h], 0.0).astype(BF16) for h in hs]
            yfull = [gy[i, h] + _dot_nt(gm[h], ss[h // 2]) for h in hs]
            sm = [jnp.where(own[h], st[h // 2], 0.0).astype(BF16) for h in hs]
            upd = [_dot(sm[h], mn[i, h].astype(BF16))
                   + mn[i, h][(1 - h % 2) * c_len:(2 - h % 2) * c_len] for h in hs]
            for p in range(PAIRS):
                pl_lanes = slice(p * LANES, (p + 1) * LANES)
                pc = pc_ref[pl.ds(pl.multiple_of(cs[i] * SUBLANES, SUBLANES), 1), pl_lanes]
                st[p] = st[p] * pc + upd[2 * p] + upd[2 * p + 1]
                yacc_ref[rows[i], pl_lanes] = jnp.where(half[0], yfull[2 * p + 1], yfull[2 * p])
        for p in range(PAIRS):
            state_ref[p] = st[p]
        return carry

    lax.fori_loop(0, n_chunks // per_trip, trip_body, 0)

    y = _swap_halves(yacc_ref[...])
    inv_n = 1.0 / HEAD_DIM
    mean = headsum(y) * inv_n
    yc = y - mean
    var = headsum(yc * yc) * inv_n
    yn = yc * lax.rsqrt(var + RWKV_GN_EPS) * gnw_ref[...] + gnb_ref[...]
    bonus = headsum(r * k2 * rk_ref[...]) * v
    y_ref[...] = ((yn + bonus) * gate).astype(y_ref.dtype)


def _rwkv(zr, batch, seq, mu, dbias, wdec, ibias, wicl, wgate, kk, ka, rk, gnw, gnb):
    tb = min(RWKV_BLOCK, seq)
    n_t = seq // tb
    idx = np.arange(tb)
    same_chunk = (idx[:, None] // RWKV_CHUNK) == (idx[None, :] // RWKV_CHUNK)
    tri = jnp.asarray(same_chunk & (idx[None, :] <= idx[:, None]), BF16)
    hid = np.arange(MXU_TILE) // HEAD_DIM
    bd = jnp.asarray(hid[:, None] == hid[None, :], BF16)
    const = lambda b, i: (0, 0)
    row = lambda b, i: (b * n_t + i, 0)
    vec = lambda n: pl.BlockSpec((1, n), const)
    full = lambda arr: pl.BlockSpec(arr.shape, const)
    return pl.pallas_call(
        _rwkv_kernel,
        grid=(batch, n_t),
        in_specs=[
            pl.BlockSpec((tb, RWKV_IN_PAD), row),
            vec(RWKV_IN_PAD), vec(WIDTH), full(wdec), vec(WIDTH), full(wicl), full(wgate),
            vec(WIDTH), vec(WIDTH), vec(WIDTH), vec(WIDTH), vec(WIDTH),
            full(tri), full(bd),
        ],
        out_specs=pl.BlockSpec((tb, WIDTH), row),
        out_shape=jax.ShapeDtypeStruct((batch * seq, WIDTH), BF16),
        scratch_shapes=[
            pltpu.VMEM((PAIRS, HEAD_DIM, LANES), F32),
            pltpu.VMEM((SUBLANES, RWKV_IN_PAD), F32),
            pltpu.VMEM((2 * tb, WIDTH), BF16),
            pltpu.VMEM((2 * tb, WIDTH), BF16),
            pltpu.VMEM((2 * tb, WIDTH), BF16),
            pltpu.VMEM((tb, WIDTH), BF16),
            pltpu.VMEM((SUBLANES * (tb // RWKV_CHUNK), WIDTH), F32),
            pltpu.VMEM((tb, WIDTH), F32),
        ],
        compiler_params=pltpu.CompilerParams(
            dimension_semantics=("arbitrary", "arbitrary"), vmem_limit_bytes=VMEM_LIMIT),
        name="rwkv7",
    )(zr, mu, dbias, wdec, ibias, wicl, wgate, kk, ka, rk, gnw, gnb, tri, bd)


def _rope(x, cos, sin_signed):
    n = x.shape[1] // LANES
    lane = lax.broadcasted_iota(jnp.int32, x.shape, 1)
    first_half = (lane % HEAD_DIM) < (HEAD_DIM // 2)
    swapped = jnp.where(first_half,
                        pltpu.roll(x, x.shape[1] - HEAD_DIM // 2, 1),
                        pltpu.roll(x, HEAD_DIM // 2, 1))
    if n > 1:
        cos = jnp.concatenate([cos] * n, axis=1)
        sin_signed = jnp.concatenate([sin_signed] * n, axis=1)
    return x * cos + swapped * sin_signed


def _fold(x, fn):
    rows, n = x.shape
    parts = x.reshape(rows // SUBLANES, SUBLANES, n)
    level = [parts[i] for i in range(rows // SUBLANES)]
    while len(level) > 1:
        nxt = [fn(level[i], level[i + 1]) for i in range(0, len(level) - 1, 2)]
        if len(level) % 2:
            nxt.append(level[-1])
        level = nxt
    return level[0]


def _heads_to_rows(x):
    return jnp.concatenate(
        [x[:, h * HEAD_DIM:(h + 1) * HEAD_DIM] for h in range(HEADS)], axis=0)


def _dsa_kernel(qq_ref, kv_ref, kiw_ref, wq_ref, cosq_ref, sinq_ref, cosk_ref, sink_ref,
                o_ref,
                k_s, ki_s, vt_s, sc_s, s_all, acc_s, *, topk, idx_scale):
    j = pl.program_id(1)
    tq = qq_ref.shape[0]
    seq = kv_ref.shape[0]
    kc = min(KEY_CHUNK, seq)
    n_ch = (j * tq + tq + kc - 1) // kc
    inf = jnp.float32(jnp.inf)

    @pl.when(j == 0)
    def _():
        kv = kv_ref[...]
        kvr = _rope(kv, cosk_ref[...], sink_ref[...])
        k_s[...] = kvr[:, :HEAD_DIM].astype(BF16)
        kvt = kv.T
        for c in range(seq // kc):
            vt_s[c] = kvt[HEAD_DIM:, c * kc:(c + 1) * kc].astype(BF16)
        kiw = kiw_ref[...]
        ki_s[...] = _rope(kiw, cosk_ref[...], sink_ref[...])[:, :HEAD_DIM].astype(BF16)

    cosq = cosq_ref[...]
    sinq = sinq_ref[...]
    q_all = _heads_to_rows(_rope(qq_ref[:, 0:WIDTH], cosq, sinq)
                           * (HEAD_DIM ** -0.5)).astype(BF16)
    qi_all = _heads_to_rows(_rope(qq_ref[:, WIDTH:2 * WIDTH], cosq, sinq)).astype(BF16)

    sel_r = lax.broadcasted_iota(jnp.int32, (2 * SUBLANES, LANES), 0)
    sel_c = lax.broadcasted_iota(jnp.int32, (2 * SUBLANES, LANES), 1)
    sel = (sel_c == sel_r + HEAD_DIM).astype(BF16)
    w1, w2, w3 = _split3(wq_ref[...])
    w_t = (_dot_nt(sel, w1) + (_dot_nt(sel, w2) + _dot_nt(sel, w3))) * idx_scale

    qpos = j * tq + lax.broadcasted_iota(jnp.int32, (1, tq), 1)
    key_in_chunk = lax.broadcasted_iota(jnp.int32, (kc, tq), 0)

    def chunk_rows(c):
        return pl.ds(pl.multiple_of(c * kc, kc), kc)

    def score_body(c, carry):
        rmax, rmin = carry
        ki = ki_s[chunk_rows(c), :]
        lg = [_dot_nt(ki, qi_all[h * tq:(h + 1) * tq]) for h in range(HEADS)]
        acc = w_t[0:1] * jnp.maximum(lg[0], 0.0)
        for h in range(1, HEADS):
            acc = acc + w_t[h:h + 1] * jnp.maximum(lg[h], 0.0)
        causal = (c * kc + key_in_chunk) <= qpos
        lo_fill = jnp.where(causal, acc, -inf)
        sc_s[chunk_rows(c), :] = lo_fill
        rmax = jnp.maximum(rmax, _fold(lo_fill, jnp.maximum))
        rmin = jnp.minimum(rmin, _fold(jnp.where(causal, acc, inf), jnp.minimum))
        return rmax, rmin

    rmax, rmin = lax.fori_loop(
        0, n_ch, score_body,
        (jnp.full((SUBLANES, tq), -inf, F32), jnp.full((SUBLANES, tq), inf, F32)))
    rmax = jnp.max(rmax, axis=0, keepdims=True)
    rmin = jnp.min(rmin, axis=0, keepdims=True)

    def count(cmp, x):
        xb = jnp.broadcast_to(x, (SUBLANES, tq))

        def body(c, accs):
            parts = sc_s[chunk_rows(c), :].reshape(kc // SUBLANES, SUBLANES, tq)
            accs = list(accs)
            for i in range(kc // SUBLANES):
                a = accs[i % COUNT_CHAINS]
                accs[i % COUNT_CHAINS] = jnp.where(cmp(parts[i], xb), a + 1.0, a)
            return tuple(accs)

        accs = lax.fori_loop(0, n_ch, body,
                             tuple(jnp.zeros((SUBLANES, tq), F32) for _ in range(COUNT_CHAINS)))
        return jnp.sum(_fold(jnp.concatenate(accs, axis=0), jnp.add), axis=0, keepdims=True)

    kf = jnp.float32(topk)
    n_causal = (qpos + 1).astype(F32)
    all_keys = n_causal < kf
    ge = lambda s, x: s >= x
    gt = lambda s, x: s > x
    zero = jnp.zeros((1, tq), F32)
    c_max = count(ge, rmax)
    c_ge0 = count(ge, zero)
    c_gt0 = count(gt, zero)
    top_tied = c_max >= kf
    zero_thr = (c_gt0 < kf) & (c_ge0 >= kf)
    above0 = c_gt0 >= kf
    lo = jnp.where(top_tied, rmax, jnp.where(zero_thr | above0, zero, rmin))
    c_lo = jnp.where(top_tied, c_max, jnp.where(zero_thr | above0, c_ge0, n_causal))
    hi = jnp.where(top_tied | above0, rmax, zero)
    done = all_keys | top_tied | zero_thr | (c_lo == kf) | (lo == hi)

    def search_cond(st):
        _, _, _, done_f, it = st
        return jnp.logical_and(jnp.min(done_f) < 0.5, it < SEARCH_ITERS_MAX)

    def search_body(st):
        lo, hi, c_lo, done_f, it = st
        for _ in range(SEARCH_UNROLL):
            active = done_f < 0.5
            mid = lo * 0.5 + hi * 0.5
            c = count(ge, mid)
            enough = c >= kf
            stuck = (mid == lo) | (mid == hi)
            lo = jnp.where(active & enough, mid, lo)
            hi = jnp.where(active & jnp.logical_not(enough), mid, hi)
            c_lo = jnp.where(active & enough, c, c_lo)
            done_f = jnp.where(active & ((c_lo == kf) | stuck), jnp.float32(1.0), done_f)
        return lo, hi, c_lo, done_f, it + SEARCH_UNROLL

    lo, hi, c_lo, _, _ = lax.while_loop(
        search_cond, search_body, (lo, hi, c_lo, done.astype(F32), jnp.int32(0)))
    lo = jnp.where(all_keys, -inf, lo)
    exact_k = all_keys | (c_lo == kf)
    no_ties = jnp.min(exact_k.astype(F32)) > 0.5

    @pl.when(no_ties)
    def _():
        def body(c, carry):
            s = sc_s[chunk_rows(c), :]
            causal = (c * kc + key_in_chunk) <= qpos
            sc_s[chunk_rows(c), :] = jnp.where((s >= lo) & causal, 0.0, MASK_BIAS)
            return carry
        lax.fori_loop(0, n_ch, body, 0)

    @pl.when(jnp.logical_not(no_ties))
    def _():
        def snap_body(c, acc):
            s = sc_s[chunk_rows(c), :]
            return jnp.minimum(acc, _fold(jnp.where(s >= lo, s, inf), jnp.minimum))
        thr = jnp.min(lax.fori_loop(0, n_ch, snap_body, jnp.full((SUBLANES, tq), inf, F32)),
                      axis=0, keepdims=True)
        need = kf - count(gt, thr)
        ki_ = lax.broadcasted_iota(jnp.int32, (kc, kc), 0)
        kj_ = lax.broadcasted_iota(jnp.int32, (kc, kc), 1)
        earlier = (kj_ < ki_).astype(BF16)
        ones = jnp.ones((2 * SUBLANES, kc), BF16)

        def body(c, run):
            s = sc_s[chunk_rows(c), :]
            causal = (c * kc + key_in_chunk) <= qpos
            eq = (s == thr).astype(BF16)
            rank = run + _dot(earlier, eq)
            take = (s > thr) | ((s == thr) & (rank < need))
            sc_s[chunk_rows(c), :] = jnp.where(take & causal, 0.0, MASK_BIAS)
            return run + _dot(ones, eq)[0:1]
        lax.fori_loop(0, n_ch, body, jnp.zeros((1, tq), F32))

    def logits_body(c, m_acc):
        kch = k_s[chunk_rows(c), :]
        bias = sc_s[chunk_rows(c), :]
        s = [_dot_nt(kch, q_all[h * tq:(h + 1) * tq]) for h in range(HEADS)]
        out = []
        for h in range(HEADS):
            sh = s[h] + bias
            s_all[chunk_rows(c), h * tq:(h + 1) * tq] = sh
            out.append(_fold(sh, jnp.maximum))
        return jnp.maximum(m_acc, jnp.concatenate(out, axis=1))

    m = jnp.max(lax.fori_loop(0, n_ch, logits_body,
                              jnp.full((SUBLANES, HEADS * tq), MASK_BIAS, F32)),
                axis=0, keepdims=True)

    def pv_body(c, l_acc):
        vt = vt_s[c]
        out = []
        for h in range(HEADS):
            cols = slice(h * tq, (h + 1) * tq)
            p = jnp.exp(s_all[chunk_rows(c), cols] - m[:, cols])
            out.append(_fold(p, jnp.add))
            acc_s[:, cols] += _dot(vt, p.astype(BF16))
        return l_acc + jnp.concatenate(out, axis=1)

    acc_s[...] = jnp.zeros_like(acc_s)
    l_acc = lax.fori_loop(0, n_ch, pv_body, jnp.zeros((SUBLANES, HEADS * tq), F32))
    out_t = acc_s[...] / jnp.sum(l_acc, axis=0, keepdims=True)
    for p in range(PAIRS):
        pair = jnp.concatenate([out_t[:, 2 * p * tq:(2 * p + 1) * tq],
                                out_t[:, (2 * p + 1) * tq:(2 * p + 2) * tq]], axis=0)
        o_ref[:, p * LANES:(p + 1) * LANES] = pair.T.astype(o_ref.dtype)


def _dsa(zd, batch, seq, cos_t, sin_t):
    tq = min(Q_TILE, seq)
    nb = seq // tq
    topk = min(TOPK_MAX, seq // 4)
    kc = min(KEY_CHUNK, seq)
    idx_scale = HEADS ** -0.5 * HEAD_DIM ** -0.5
    kernel = functools.partial(_dsa_kernel, topk=topk, idx_scale=idx_scale)
    qcols = 2 * WIDTH // LANES
    return pl.pallas_call(
        kernel,
        grid=(batch, nb),
        in_specs=[
            pl.BlockSpec((tq, 2 * WIDTH), lambda b, j: (b * nb + j, 0)),
            pl.BlockSpec((seq, LANES), lambda b, j: (b, qcols)),
            pl.BlockSpec((seq, LANES), lambda b, j: (b, qcols + 1)),
            pl.BlockSpec((tq, LANES), lambda b, j: (b * nb + j, qcols + 1)),
            pl.BlockSpec((tq, LANES), lambda b, j: (j, 0)),
            pl.BlockSpec((tq, LANES), lambda b, j: (j, 0)),
            pl.BlockSpec((seq, LANES), lambda b, j: (0, 0)),
            pl.BlockSpec((seq, LANES), lambda b, j: (0, 0)),
        ],
        out_specs=pl.BlockSpec((tq, WIDTH), lambda b, j: (b * nb + j, 0)),
        out_shape=jax.ShapeDtypeStruct((batch * seq, WIDTH), BF16),
        scratch_shapes=[
            pltpu.VMEM((seq, HEAD_DIM), BF16),
            pltpu.VMEM((seq, HEAD_DIM), BF16),
            pltpu.VMEM((seq // kc, HEAD_DIM, kc), BF16),
            pltpu.VMEM((seq, tq), F32),
            pltpu.VMEM((seq, HEADS * tq), F32),
            pltpu.VMEM((HEAD_DIM, HEADS * tq), F32),
        ],
        compiler_params=pltpu.CompilerParams(
            dimension_semantics=("arbitrary", "arbitrary"), vmem_limit_bytes=VMEM_LIMIT),
        name="dsa",
    )(zd, zd, zd, zd, cos_t, sin_t, cos_t, sin_t)


def _merge_ffn_kernel(x_ref, ya_ref, yb_ref, zg_ref, wba_ref, wbb_ref, wout_ref, gffn_ref,
                      wup_ref, wdown_ref, gfin_ref, o_ref, *, final, hidden_chunk):
    d = x_ref.shape[1]
    pa = _dot(ya_ref[...], wba_ref[...])
    pb = _dot(yb_ref[...], wbb_ref[...])
    merged = _sigmoid(zg_ref[:, 0:d]) * pa + _sigmoid(zg_ref[:, d:2 * d]) * pb
    x1 = x_ref[...] + _dot(merged.astype(BF16), wout_ref[...])
    h2 = _rms(x1, gffn_ref[...]).astype(BF16)
    x2 = x1
    hidden = wup_ref.shape[1]
    for c in range(hidden // hidden_chunk):
        sl = slice(c * hidden_chunk, (c + 1) * hidden_chunk)
        u = jnp.maximum(_dot(h2, wup_ref[:, sl]), 0.0)
        x2 = x2 + _dot((u * u).astype(BF16), wdown_ref[sl, :])
    o_ref[...] = _rms(x2, gfin_ref[...]) if final else x2


def _merge_ffn(x2, ya, yb, zg, wba, wbb, wout, gffn, wup, wdown, gfin, tm, final):
    n, d = x2.shape
    const = lambda i: (0, 0)
    row = lambda i: (i, 0)
    resident = lambda arr: pl.BlockSpec(arr.shape, const, pipeline_mode=pl.Buffered(1))
    kernel = functools.partial(_merge_ffn_kernel, final=final,
                               hidden_chunk=min(1024, wup.shape[1]))
    return pl.pallas_call(
        kernel,
        grid=(n // tm,),
        in_specs=[
            pl.BlockSpec((tm, d), row),
            pl.BlockSpec((tm, WIDTH), row),
            pl.BlockSpec((tm, WIDTH), row),
            pl.BlockSpec((tm, 2 * d), row),
            resident(wba), resident(wbb), resident(wout),
            pl.BlockSpec((1, d), const),
            resident(wup), resident(wdown),
            pl.BlockSpec((1, d), const),
        ],
        out_specs=pl.BlockSpec((tm, d), row),
        out_shape=jax.ShapeDtypeStruct((n, d), F32),
        compiler_params=pltpu.CompilerParams(
            dimension_semantics=("arbitrary",), vmem_limit_bytes=VMEM_LIMIT),
        name="merge_ffn",
    )(x2, ya, yb, zg, wba, wbb, wout, gffn, wup, wdown, gfin)


def _rope_tables(seq):
    inv = 1.0 / (ROPE_THETA ** (jnp.arange(0, HEAD_DIM, 2, dtype=F32) / HEAD_DIM))
    ang = jnp.arange(seq, dtype=F32)[:, None] * inv[None, :]
    cos, sin = jnp.cos(ang), jnp.sin(ang)
    cos_t = jnp.concatenate([cos, cos] * (LANES // HEAD_DIM), axis=1)
    sin_t = jnp.concatenate([-sin, sin] * (LANES // HEAD_DIM), axis=1)
    return cos_t, sin_t


def _pad_cols(a, n):
    return jnp.pad(a, ((0, 0), (0, n - a.shape[1])))


def _pad_rows(a, n):
    return jnp.pad(a, ((0, n - a.shape[0]), (0, 0)))


def kernel(x, g_mix, w_in, mu_shift, decay_bias, w_decay_up, iclr_bias, w_iclr_up, w_gate_up,
           k_k, k_a, r_k, gn_w, gn_b, w_branch, w_out, g_ffn, w_ffn_up, w_ffn_down, g_final):
    batch, seq, d = x.shape
    depth = g_mix.shape[0]
    n = batch * seq
    tm = 256
    assert seq % min(Q_TILE, seq) == 0 and seq % min(KEY_CHUNK, seq) == 0
    assert seq % min(RWKV_BLOCK, seq) == 0 and min(RWKV_BLOCK, seq) % RWKV_CHUNK == 0
    assert n % tm == 0
    cos_t, sin_t = _rope_tables(seq)
    row = lambda a: a.reshape(1, -1)
    xf = x.reshape(n, d)
    for l in range(depth):
        w = w_in[l]
        w_rwkv = _pad_cols(w[:, :RWKV_IN], RWKV_IN_PAD).astype(BF16)
        wd = w[:, RWKV_IN:RWKV_IN + DSA_IN]
        o_k, o_v, o_qi = WIDTH, WIDTH + HEAD_DIM, WIDTH + 2 * HEAD_DIM
        o_ki, o_wi = o_qi + WIDTH, o_qi + WIDTH + HEAD_DIM
        w_dsa = _pad_cols(jnp.concatenate(
            [wd[:, :WIDTH], wd[:, o_qi:o_ki], wd[:, o_k:o_v], wd[:, o_v:o_qi],
             wd[:, o_ki:o_wi], wd[:, o_wi:]], axis=1), DSA_IN_PAD).astype(BF16)
        w_gate = w[:, RWKV_IN + DSA_IN:].astype(BF16)
        zr, zd, zg = _inproj(xf, row(g_mix[l]), w_rwkv, w_dsa, w_gate, tm)

        wdec = _pad_rows(w_decay_up[l], LANES).astype(BF16)
        wicl = jnp.pad(w_iclr_up[l], ((DECAY_LORA, 0), (0, 0))).astype(BF16)
        wgate = _pad_rows(w_gate_up[l], GATE_LORA_PAD).astype(BF16)
        mu = _pad_cols(row(mu_shift[l]), RWKV_IN_PAD)
        ya = _rwkv(zr, batch, seq, mu, row(decay_bias[l]), wdec, row(iclr_bias[l]), wicl,
                   wgate, row(k_k[l]), row(k_a[l]), row(r_k[l]), row(gn_w[l]), row(gn_b[l]))
        yb = _dsa(zd, batch, seq, cos_t, sin_t)

        xf = _merge_ffn(xf, ya, yb, zg, w_branch[l, 0].astype(BF16),
                        w_branch[l, 1].astype(BF16), w_out[l].astype(BF16), row(g_ffn[l]),
                        w_ffn_up[l].astype(BF16), w_ffn_down[l].astype(BF16), row(g_final),
                        tm, final=(l == depth - 1))
    return xf.reshape(batch, seq, d)
```

```python
import functools

import numpy as np
import jax
import jax.numpy as jnp
from jax import lax
from jax.experimental import pallas as pl
from jax.experimental.pallas import tpu as pltpu

F32 = jnp.float32
BF16 = jnp.bfloat16

HEADS = 8
HEAD_DIM = 64
PAIRS = HEADS // 2
WIDTH = HEADS * HEAD_DIM
DECAY_LORA = 64
ICLR_LORA = 64
GATE_LORA = 160
GATE_LORA_PAD = 256
RWKV_IN = 3 * WIDTH + DECAY_LORA + ICLR_LORA + GATE_LORA
RWKV_IN_PAD = 3 * WIDTH + DECAY_LORA + ICLR_LORA + GATE_LORA_PAD
DSA_IN = WIDTH + 2 * HEAD_DIM + WIDTH + HEAD_DIM + HEADS
DSA_IN_PAD = 2 * WIDTH + 2 * 128
TOPK_MAX = 256
Q_TILE = 256
ROPE_THETA = 10000.0
NORM_EPS = 1e-6
RWKV_GN_EPS = 64e-5
L2_EPS = 1e-12

LANES = 128
SUBLANES = 8
MXU_TILE = 256
VMEM_LIMIT = 56 * 1024 * 1024

RWKV_CHUNK = 64
RWKV_BLOCK = 256
RWKV_GROUPS = 1
KEY_CHUNK = 256
SEARCH_UNROLL = 4
SEARCH_ITERS_MAX = 320
COUNT_CHAINS = 4
MASK_BIAS = -1e30


def _dot(a, b):
    return jnp.dot(a, b, preferred_element_type=F32)


def _dot_nt(a, b):
    return lax.dot_general(a, b, (((1,), (1,)), ((), ())), preferred_element_type=F32)


def _dot_tn(a, b):
    return lax.dot_general(a, b, (((0,), (0,)), ((), ())), preferred_element_type=F32)


def _split2(x):
    hi = x.astype(BF16)
    lo = (x - hi.astype(F32)).astype(BF16)
    return hi, lo


def _split3(x):
    h1 = x.astype(BF16)
    r1 = x - h1.astype(F32)
    h2 = r1.astype(BF16)
    h3 = (r1 - h2.astype(F32)).astype(BF16)
    return h1, h2, h3


def _rms(x, g):
    ms = jnp.mean(x * x, axis=-1, keepdims=True)
    return x * lax.rsqrt(ms + NORM_EPS) * g


def _sigmoid(x):
    return 1.0 / (1.0 + jnp.exp(-x))


def _swap_halves(x):
    cols = [pltpu.roll(x[:, g * LANES:(g + 1) * LANES], HEAD_DIM, 1)
            for g in range(x.shape[1] // LANES)]
    return jnp.concatenate(cols, axis=1)


def _inproj_kernel(x_ref, g_ref, wr_ref, wd_ref, wg_ref, zr_ref, zd_ref, zg_ref):
    h = _rms(x_ref[...], g_ref[...]).astype(BF16)
    zr_ref[...] = _dot(h, wr_ref[...])
    zd_ref[...] = _dot(h, wd_ref[...])
    zg_ref[...] = _dot(h, wg_ref[...])


def _inproj(x2, g, wr, wd, wg, tm):
    n, d = x2.shape
    const = lambda i: (0, 0)
    row = lambda i: (i, 0)
    return pl.pallas_call(
        _inproj_kernel,
        grid=(n // tm,),
        in_specs=[
            pl.BlockSpec((tm, d), row),
            pl.BlockSpec((1, d), const),
            pl.BlockSpec(wr.shape, const),
            pl.BlockSpec(wd.shape, const),
            pl.BlockSpec(wg.shape, const),
        ],
        out_specs=[
            pl.BlockSpec((tm, wr.shape[1]), row),
            pl.BlockSpec((tm, wd.shape[1]), row),
            pl.BlockSpec((tm, wg.shape[1]), row),
        ],
        out_shape=[
            jax.ShapeDtypeStruct((n, wr.shape[1]), F32),
            jax.ShapeDtypeStruct((n, wd.shape[1]), F32),
            jax.ShapeDtypeStruct((n, wg.shape[1]), F32),
        ],
        compiler_params=pltpu.CompilerParams(
            dimension_semantics=("arbitrary",), vmem_limit_bytes=VMEM_LIMIT),
        name="inproj",
    )(x2, g, wr, wd, wg)


def _rwkv_kernel(z_ref, mu_ref, dbias_ref, wdec_ref, ibias_ref, wicl_ref, wgate_ref,
                 kk_ref, ka_ref, rk_ref, gnw_ref, gnb_ref, tri_ref, bd_ref,
                 y_ref,
                 state_ref, carry_ref, qa_ref, bk_ref, bkh_ref, vsw_ref, pc_ref, yacc_ref):
    tb = z_ref.shape[0]
    c_len = RWKV_CHUNK
    n_chunks = tb // c_len

    @pl.when(pl.program_id(1) == 0)
    def _():
        state_ref[...] = jnp.zeros_like(state_ref)
        carry_ref[...] = jnp.zeros_like(carry_ref)

    n_groups = RWKV_GROUPS
    per_group = n_chunks // n_groups
    g_rows = per_group * c_len
    bd = bd_ref[...]
    bd_w = bd.shape[0]

    def headsum(t):
        tb16 = t.astype(BF16)
        return jnp.concatenate(
            [_dot(tb16[:, i:i + bd_w], bd) for i in range(0, WIDTH, bd_w)], axis=1)

    def prologue_parts(g, t):
        r0 = g * g_rows

        def part_a():
            z = z_ref[r0:r0 + g_rows, :]
            before = carry_ref[0:1, :] if g == 0 else z_ref[r0 - 1:r0, :]
            first_row = lax.broadcasted_iota(jnp.int32, (g_rows, 1), 0) == 0
            prev = jnp.where(first_row, before, pltpu.roll(z, 1, 0))
            zs = z + (prev - z) * mu_ref[...]
            t["r"] = zs[:, 0:WIDTH]
            t["k"] = zs[:, WIDTH:2 * WIDTH]
            t["v"] = zs[:, 2 * WIDTH:3 * WIDTH]
            lora_in = zs[:, 3 * WIDTH:3 * WIDTH + LANES]
            gate_in = zs[:, 3 * WIDTH + LANES:]
            dec_pre = dbias_ref[...] + _dot(jnp.tanh(lora_in).astype(BF16), wdec_ref[...])
            neg = -dec_pre
            softplus = jnp.maximum(neg, 0.0) + jnp.log1p(jnp.exp(-jnp.abs(neg)))
            w_log = -softplus - 0.5
            t["logw"] = -jnp.exp(w_log)
            t["iclr"] = _sigmoid(ibias_ref[...] + _dot(lora_in.astype(BF16), wicl_ref[...]))
            t["gate"] = _dot(_sigmoid(gate_in).astype(BF16), wgate_ref[...])
            vsw_ref[r0:r0 + g_rows, :] = _swap_halves(t["v"]).astype(BF16)

        def part_b():
            k, iclr, logw = t["k"], t["iclr"], t["logw"]
            kk = k * kk_ref[...]
            kk = kk / jnp.maximum(jnp.sqrt(headsum(kk * kk)), L2_EPS)
            t["k2"] = k * (1.0 + (iclr - 1.0) * ka_ref[...])
            t["a"] = -kk
            t["b"] = kk * iclr
            logw_hi, logw_lo = _split2(logw)
            tri = tri_ref[0:g_rows, 0:g_rows]
            t["cum"] = _dot(tri, logw_hi) + _dot(tri, logw_lo)

        def part_c():
            cum, logw, r, k2, a, b = t["cum"], t["logw"], t["r"], t["k2"], t["a"], t["b"]
            ends = [cum[(i + 1) * c_len - 1:(i + 1) * c_len, :] for i in range(per_group)]
            cum_end = jnp.concatenate([jnp.broadcast_to(e, (c_len, WIDTH)) for e in ends], axis=0)
            e_neg = jnp.exp(-cum)
            e_tail = jnp.exp(cum_end - cum)
            at = (a * jnp.exp(cum - logw)).astype(BF16)
            rt = (r * jnp.exp(cum)).astype(BF16)
            bt = (b * e_neg).astype(BF16)
            kt = (k2 * e_neg).astype(BF16)
            bh = (b * e_tail).astype(BF16)
            kh = (k2 * e_tail).astype(BF16)
            for i in range(per_group):
                c = g * per_group + i
                src = slice(i * c_len, (i + 1) * c_len)
                top = slice(2 * c * c_len, (2 * c + 1) * c_len)
                bot = slice((2 * c + 1) * c_len, (2 * c + 2) * c_len)
                qa_ref[top, :] = at[src]
                qa_ref[bot, :] = rt[src]
                bk_ref[top, :] = bt[src]
                bk_ref[bot, :] = kt[src]
                bkh_ref[top, :] = bh[src]
                bkh_ref[bot, :] = kh[src]
                pc_ref[c * SUBLANES:c * SUBLANES + 1, :] = jnp.exp(ends[i])
            t["rkr"] = r * k2 * rk_ref[...]

        return [part_a, part_b, part_c]

    ri = lax.broadcasted_iota(jnp.int32, (c_len, 2 * c_len), 0)
    ci = lax.broadcasted_iota(jnp.int32, (c_len, 2 * c_len), 1) % c_len
    strict = ci < ri
    incl = ci <= ri
    lane = lax.broadcasted_iota(jnp.int32, (1, LANES), 1)
    half = [lane < HEAD_DIM, lane >= HEAD_DIM]
    zero_bf = jnp.zeros((), BF16)
    zero_top = jnp.zeros((c_len, LANES), BF16)
    hs = range(HEADS)
    lanes = [slice((h // 2) * LANES, (h // 2 + 1) * LANES) for h in hs]
    own = [half[h % 2] for h in hs]
    other = [half[1 - h % 2] for h in hs]
    steps = int(np.log2(c_len))

    def independent(g, hooks):
        cs = [g * per_group + i for i in range(per_group)]
        items = [(c, h) for c in cs for h in hs]
        rows2 = {c: slice(2 * c * c_len, (2 * c + 2) * c_len) for c in cs}
        rows = {c: slice(c * c_len, (c + 1) * c_len) for c in cs}
        qa = {(c, h): jnp.where(own[h], qa_ref[rows2[c], lanes[h]], zero_bf) for c, h in items}
        ab = {(c, h): _dot_nt(qa[c, h], bk_ref[rows2[c], lanes[h]]) for c, h in items}
        mtop = {k: jnp.where(strict, ab[k][:c_len], 0.0) for k in items}
        mbot = {k: jnp.where(incl, ab[k][c_len:], 0.0).astype(BF16) for k in items}
        vp = {(c, h): jnp.where(other[h], vsw_ref[rows[c], lanes[h]], zero_bf) for c, h in items}
        lkv = {k: _dot(mtop[k].astype(BF16), jnp.concatenate([zero_top, vp[k]], axis=0))
               for k in items}
        x = {k: qa[k][:c_len].astype(F32) + lkv[k] for k in items}
        pw = {k: mtop[k][:, :c_len].astype(BF16) for k in items}
        for j in range(steps):
            x = {k: x[k] + _dot(pw[k], x[k].astype(BF16)) for k in items}
            if j + 1 < steps:
                pw = {k: _dot(pw[k], pw[k]).astype(BF16) for k in items}
            for hook in hooks[j]:
                hook()
        rhs = {k: jnp.concatenate([x[k].astype(BF16), vp[k]], axis=0) for k in items}
        gy = {k: _dot(mbot[k], rhs[k]) for k in items}
        mn = {(c, h): _dot_tn(rhs[c, h],
                              jnp.where(own[h], bkh_ref[rows2[c], lanes[h]], zero_bf))
              for c, h in items}
        gr = {k: qa[k][c_len:].astype(F32) + gy[k] for k in items}
        return gy, mn, gr

    st = [state_ref[p] for p in range(PAIRS)]

    def dependent(c, res):
        gy, mn, gr = res
        sm = [jnp.where(own[h], st[h // 2], 0.0).astype(BF16) for h in hs]
        upd = [_dot(sm[h], mn[c, h].astype(BF16))
               + mn[c, h][(1 - h % 2) * c_len:(2 - h % 2) * c_len] for h in hs]
        ss = [jnp.concatenate([st[p].astype(BF16)] * 2, axis=0) for p in range(PAIRS)]
        gm = [jnp.where(own[h], gr[c, h], 0.0).astype(BF16) for h in hs]
        yfull = [gy[c, h] + _dot_nt(gm[h], ss[h // 2]) for h in hs]
        for p in range(PAIRS):
            pl_lanes = slice(p * LANES, (p + 1) * LANES)
            pc = pc_ref[c * SUBLANES:c * SUBLANES + 1, pl_lanes]
            st[p] = st[p] * pc + upd[2 * p] + upd[2 * p + 1]
            yacc_ref[c * c_len:(c + 1) * c_len, pl_lanes] = jnp.where(
                half[0], yfull[2 * p + 1], yfull[2 * p])

    tails = [dict() for _ in range(n_groups)]
    for part in prologue_parts(0, tails[0]):
        part()
    prev_res = None
    for g in range(n_groups):
        hooks = {j: [] for j in range(steps)}
        if g + 1 < n_groups:
            for n, part in enumerate(prologue_parts(g + 1, tails[g + 1])):
                hooks[min(steps - 1, 2 * n)].append(part)
        if prev_res is not None:
            for i in range(per_group):
                at_step = min(steps - 1, (i + 1) * steps // (per_group + 1))
                hooks[at_step].append(
                    functools.partial(dependent, (g - 1) * per_group + i, prev_res))
        prev_res = independent(g, hooks)
    for i in range(per_group):
        dependent((n_groups - 1) * per_group + i, prev_res)
    for p in range(PAIRS):
        state_ref[p] = st[p]
    carry_ref[0:1, :] = z_ref[tb - 1:tb, :]

    rkr = jnp.concatenate([t["rkr"] for t in tails], axis=0)
    v = jnp.concatenate([t["v"] for t in tails], axis=0)
    gate = jnp.concatenate([t["gate"] for t in tails], axis=0)
    y = _swap_halves(yacc_ref[...])
    inv_n = 1.0 / HEAD_DIM
    mean = headsum(y) * inv_n
    yc = y - mean
    var = headsum(yc * yc) * inv_n
    yn = yc * lax.rsqrt(var + RWKV_GN_EPS) * gnw_ref[...] + gnb_ref[...]
    bonus = headsum(rkr) * v
    y_ref[...] = ((yn + bonus) * gate).astype(y_ref.dtype)


def _rwkv(zr, batch, seq, mu, dbias, wdec, ibias, wicl, wgate, kk, ka, rk, gnw, gnb):
    tb = min(RWKV_BLOCK, seq)
    n_t = seq // tb
    idx = np.arange(tb)
    same_chunk = (idx[:, None] // RWKV_CHUNK) == (idx[None, :] // RWKV_CHUNK)
    tri = jnp.asarray(same_chunk & (idx[None, :] <= idx[:, None]), BF16)
    hid = np.arange(MXU_TILE) // HEAD_DIM
    bd = jnp.asarray(hid[:, None] == hid[None, :], BF16)
    const = lambda b, i: (0, 0)
    row = lambda b, i: (b * n_t + i, 0)
    vec = lambda n: pl.BlockSpec((1, n), const)
    full = lambda arr: pl.BlockSpec(arr.shape, const)
    return pl.pallas_call(
        _rwkv_kernel,
        grid=(batch, n_t),
        in_specs=[
            pl.BlockSpec((tb, RWKV_IN_PAD), row),
            vec(RWKV_IN_PAD), vec(WIDTH), full(wdec), vec(WIDTH), full(wicl), full(wgate),
            vec(WIDTH), vec(WIDTH), vec(WIDTH), vec(WIDTH), vec(WIDTH),
            full(tri), full(bd),
        ],
        out_specs=pl.BlockSpec((tb, WIDTH), row),
        out_shape=jax.ShapeDtypeStruct((batch * seq, WIDTH), BF16),
        scratch_shapes=[
            pltpu.VMEM((PAIRS, HEAD_DIM, LANES), F32),
            pltpu.VMEM((SUBLANES, RWKV_IN_PAD), F32),
            pltpu.VMEM((2 * tb, WIDTH), BF16),
            pltpu.VMEM((2 * tb, WIDTH), BF16),
            pltpu.VMEM((2 * tb, WIDTH), BF16),
            pltpu.VMEM((tb, WIDTH), BF16),
            pltpu.VMEM((SUBLANES * (tb // RWKV_CHUNK), WIDTH), F32),
            pltpu.VMEM((tb, WIDTH), F32),
        ],
        compiler_params=pltpu.CompilerParams(
            dimension_semantics=("arbitrary", "arbitrary"), vmem_limit_bytes=VMEM_LIMIT),
        name="rwkv7",
    )(zr, mu, dbias, wdec, ibias, wicl, wgate, kk, ka, rk, gnw, gnb, tri, bd)


def _rope(x, cos, sin_signed):
    lane = lax.broadcasted_iota(jnp.int32, (x.shape[0], LANES), 1)
    first_half = (lane % HEAD_DIM) < (HEAD_DIM // 2)
    out = []
    for g in range(x.shape[1] // LANES):
        xg = x[:, g * LANES:(g + 1) * LANES]
        swapped = jnp.where(first_half,
                            pltpu.roll(xg, LANES - HEAD_DIM // 2, 1),
                            pltpu.roll(xg, HEAD_DIM // 2, 1))
        out.append(xg * cos + swapped * sin_signed)
    return out[0] if len(out) == 1 else jnp.concatenate(out, axis=1)


def _fold(x, fn):
    rows, n = x.shape
    parts = x.reshape(rows // SUBLANES, SUBLANES, n)
    level = [parts[i] for i in range(rows // SUBLANES)]
    while len(level) > 1:
        nxt = [fn(level[i], level[i + 1]) for i in range(0, len(level) - 1, 2)]
        if len(level) % 2:
            nxt.append(level[-1])
        level = nxt
    return level[0]


def _heads_to_rows(x):
    return jnp.concatenate(
        [x[:, h * HEAD_DIM:(h + 1) * HEAD_DIM] for h in range(HEADS)], axis=0)


def _dsa_kernel(qq_ref, kv_ref, kiw_ref, wq_ref, cosq_ref, sinq_ref, cosk_ref, sink_ref,
                o_ref,
                k_s, ki_s, vt_s, sc_s, s_all, acc_s, *, topk, idx_scale):
    j = pl.program_id(1)
    tq = qq_ref.shape[0]
    seq = kv_ref.shape[0]
    kc = min(KEY_CHUNK, seq)
    n_ch = (j * tq + tq + kc - 1) // kc
    inf = jnp.float32(jnp.inf)

    @pl.when(j == 0)
    def _():
        kv = kv_ref[...]
        kvr = _rope(kv, cosk_ref[...], sink_ref[...])
        k_s[...] = kvr[:, :HEAD_DIM].astype(BF16)
        kvt = kv.T
        for c in range(seq // kc):
            vt_s[c] = kvt[HEAD_DIM:, c * kc:(c + 1) * kc].astype(BF16)
        kiw = kiw_ref[...]
        ki_s[...] = _rope(kiw, cosk_ref[...], sink_ref[...])[:, :HEAD_DIM].astype(BF16)

    cosq = cosq_ref[...]
    sinq = sinq_ref[...]
    q_all = _heads_to_rows(_rope(qq_ref[:, 0:WIDTH], cosq, sinq)
                           * (HEAD_DIM ** -0.5)).astype(BF16)
    qi_all = _heads_to_rows(_rope(qq_ref[:, WIDTH:2 * WIDTH], cosq, sinq)).astype(BF16)

    sel_r = lax.broadcasted_iota(jnp.int32, (2 * SUBLANES, LANES), 0)
    sel_c = lax.broadcasted_iota(jnp.int32, (2 * SUBLANES, LANES), 1)
    sel = (sel_c == sel_r + HEAD_DIM).astype(BF16)
    w1, w2, w3 = _split3(wq_ref[...])
    w_t = (_dot_nt(sel, w1) + (_dot_nt(sel, w2) + _dot_nt(sel, w3))) * idx_scale

    qpos = j * tq + lax.broadcasted_iota(jnp.int32, (1, tq), 1)
    key_in_chunk = lax.broadcasted_iota(jnp.int32, (kc, tq), 0)

    def chunk_rows(c):
        return pl.ds(pl.multiple_of(c * kc, kc), kc)

    def paired_loop(step, carry):
        carry = lax.fori_loop(0, n_ch // 2, lambda t, cr: step([2 * t, 2 * t + 1], cr), carry)
        return lax.fori_loop(0, n_ch % 2, lambda _, cr: step([n_ch - 1], cr), carry)

    def score_step(cs, carry):
        rmax, rmin = carry
        lg = [[_dot_nt(ki_s[chunk_rows(c), :], qi_all[h * tq:(h + 1) * tq])
               for h in range(HEADS)] for c in cs]
        for i, c in enumerate(cs):
            acc = w_t[0:1] * jnp.maximum(lg[i][0], 0.0)
            for h in range(1, HEADS):
                acc = acc + w_t[h:h + 1] * jnp.maximum(lg[i][h], 0.0)
            causal = (c * kc + key_in_chunk) <= qpos
            lo_fill = jnp.where(causal, acc, -inf)
            sc_s[chunk_rows(c), :] = lo_fill
            rmax = jnp.maximum(rmax, _fold(lo_fill, jnp.maximum))
            rmin = jnp.minimum(rmin, _fold(jnp.where(causal, acc, inf), jnp.minimum))
        return rmax, rmin

    rmax, rmin = paired_loop(
        score_step,
        (jnp.full((SUBLANES, tq), -inf, F32), jnp.full((SUBLANES, tq), inf, F32)))
    rmax = jnp.max(rmax, axis=0, keepdims=True)
    rmin = jnp.min(rmin, axis=0, keepdims=True)

    def count(cmp, x):
        xb = jnp.broadcast_to(x, (SUBLANES, tq))

        def body(c, accs):
            parts = sc_s[chunk_rows(c), :].reshape(kc // SUBLANES, SUBLANES, tq)
            accs = list(accs)
            for i in range(kc // SUBLANES):
                a = accs[i % COUNT_CHAINS]
                accs[i % COUNT_CHAINS] = jnp.where(cmp(parts[i], xb), a + 1.0, a)
            return tuple(accs)

        accs = lax.fori_loop(0, n_ch, body,
                             tuple(jnp.zeros((SUBLANES, tq), F32) for _ in range(COUNT_CHAINS)))
        return jnp.sum(_fold(jnp.concatenate(accs, axis=0), jnp.add), axis=0, keepdims=True)

    kf = jnp.float32(topk)
    n_causal = (qpos + 1).astype(F32)
    all_keys = n_causal < kf
    ge = lambda s, x: s >= x
    gt = lambda s, x: s > x
    zero = jnp.zeros((1, tq), F32)
    c_max = count(ge, rmax)
    c_ge0 = count(ge, zero)
    c_gt0 = count(gt, zero)
    top_tied = c_max >= kf
    zero_thr = (c_gt0 < kf) & (c_ge0 >= kf)
    above0 = c_gt0 >= kf
    lo = jnp.where(top_tied, rmax, jnp.where(zero_thr | above0, zero, rmin))
    c_lo = jnp.where(top_tied, c_max, jnp.where(zero_thr | above0, c_ge0, n_causal))
    hi = jnp.where(top_tied | above0, rmax, zero)
    done = all_keys | top_tied | zero_thr | (c_lo == kf) | (lo == hi)

    def search_cond(st):
        _, _, _, done_f, it = st
        return jnp.logical_and(jnp.min(done_f) < 0.5, it < SEARCH_ITERS_MAX)

    def search_body(st):
        lo, hi, c_lo, done_f, it = st
        for _ in range(SEARCH_UNROLL):
            active = done_f < 0.5
            mid = lo * 0.5 + hi * 0.5
            c = count(ge, mid)
            enough = c >= kf
            stuck = (mid == lo) | (mid == hi)
            lo = jnp.where(active & enough, mid, lo)
            hi = jnp.where(active & jnp.logical_not(enough), mid, hi)
            c_lo = jnp.where(active & enough, c, c_lo)
            done_f = jnp.where(active & ((c_lo == kf) | stuck), jnp.float32(1.0), done_f)
        return lo, hi, c_lo, done_f, it + SEARCH_UNROLL

    lo, hi, c_lo, _, _ = lax.while_loop(
        search_cond, search_body, (lo, hi, c_lo, done.astype(F32), jnp.int32(0)))
    lo = jnp.where(all_keys, -inf, lo)
    exact_k = all_keys | (c_lo == kf)
    no_ties = jnp.min(exact_k.astype(F32)) > 0.5

    @pl.when(no_ties)
    def _():
        def body(c, carry):
            s = sc_s[chunk_rows(c), :]
            causal = (c * kc + key_in_chunk) <= qpos
            sc_s[chunk_rows(c), :] = jnp.where((s >= lo) & causal, 0.0, MASK_BIAS)
            return carry
        lax.fori_loop(0, n_ch, body, 0)

    @pl.when(jnp.logical_not(no_ties))
    def _():
        def snap_body(c, acc):
            s = sc_s[chunk_rows(c), :]
            return jnp.minimum(acc, _fold(jnp.where(s >= lo, s, inf), jnp.minimum))
        thr = jnp.min(lax.fori_loop(0, n_ch, snap_body, jnp.full((SUBLANES, tq), inf, F32)),
                      axis=0, keepdims=True)
        need = kf - count(gt, thr)
        ki_ = lax.broadcasted_iota(jnp.int32, (kc, kc), 0)
        kj_ = lax.broadcasted_iota(jnp.int32, (kc, kc), 1)
        earlier = (kj_ < ki_).astype(BF16)
        ones = jnp.ones((2 * SUBLANES, kc), BF16)

        def body(c, run):
            s = sc_s[chunk_rows(c), :]
            causal = (c * kc + key_in_chunk) <= qpos
            eq = (s == thr).astype(BF16)
            rank = run + _dot(earlier, eq)
            take = (s > thr) | ((s == thr) & (rank < need))
            sc_s[chunk_rows(c), :] = jnp.where(take & causal, 0.0, MASK_BIAS)
            return run + _dot(ones, eq)[0:1]
        lax.fori_loop(0, n_ch, body, jnp.zeros((1, tq), F32))

    def logits_step(cs, m_acc):
        s = [[_dot_nt(k_s[chunk_rows(c), :], q_all[h * tq:(h + 1) * tq])
              for h in range(HEADS)] for c in cs]
        for i, c in enumerate(cs):
            bias = sc_s[chunk_rows(c), :]
            out = []
            for h in range(HEADS):
                sh = s[i][h] + bias
                s_all[chunk_rows(c), h * tq:(h + 1) * tq] = sh
                out.append(_fold(sh, jnp.maximum))
            m_acc = jnp.maximum(m_acc, jnp.concatenate(out, axis=1))
        return m_acc

    m = jnp.max(paired_loop(logits_step, jnp.full((SUBLANES, HEADS * tq), MASK_BIAS, F32)),
                axis=0, keepdims=True)

    def pv_step(cs, l_acc):
        out = []
        for h in range(HEADS):
            cols = slice(h * tq, (h + 1) * tq)
            p = [jnp.exp(s_all[chunk_rows(c), cols] - m[:, cols]) for c in cs]
            fold = _fold(p[0], jnp.add)
            upd = _dot(vt_s[cs[0]], p[0].astype(BF16))
            for i in range(1, len(cs)):
                fold = fold + _fold(p[i], jnp.add)
                upd = upd + _dot(vt_s[cs[i]], p[i].astype(BF16))
            out.append(fold)
            acc_s[:, cols] += upd
        return l_acc + jnp.concatenate(out, axis=1)

    acc_s[...] = jnp.zeros_like(acc_s)
    l_acc = paired_loop(pv_step, jnp.zeros((SUBLANES, HEADS * tq), F32))
    out_t = acc_s[...] / jnp.sum(l_acc, axis=0, keepdims=True)
    for p in range(PAIRS):
        pair = jnp.concatenate([out_t[:, 2 * p * tq:(2 * p + 1) * tq],
                                out_t[:, (2 * p + 1) * tq:(2 * p + 2) * tq]], axis=0)
        o_ref[:, p * LANES:(p + 1) * LANES] = pair.T.astype(o_ref.dtype)


def _dsa(zd, batch, seq, cos_t, sin_t):
    tq = min(Q_TILE, seq)
    nb = seq // tq
    topk = min(TOPK_MAX, seq // 4)
    kc = min(KEY_CHUNK, seq)
    idx_scale = HEADS ** -0.5 * HEAD_DIM ** -0.5
    kernel = functools.partial(_dsa_kernel, topk=topk, idx_scale=idx_scale)
    qcols = 2 * WIDTH // LANES
    return pl.pallas_call(
        kernel,
        grid=(batch, nb),
        in_specs=[
            pl.BlockSpec((tq, 2 * WIDTH), lambda b, j: (b * nb + j, 0)),
            pl.BlockSpec((seq, LANES), lambda b, j: (b, qcols)),
            pl.BlockSpec((seq, LANES), lambda b, j: (b, qcols + 1)),
            pl.BlockSpec((tq, LANES), lambda b, j: (b * nb + j, qcols + 1)),
            pl.BlockSpec((tq, LANES), lambda b, j: (j, 0)),
            pl.BlockSpec((tq, LANES), lambda b, j: (j, 0)),
            pl.BlockSpec((seq, LANES), lambda b, j: (0, 0)),
            pl.BlockSpec((seq, LANES), lambda b, j: (0, 0)),
        ],
        out_specs=pl.BlockSpec((tq, WIDTH), lambda b, j: (b * nb + j, 0)),
        out_shape=jax.ShapeDtypeStruct((batch * seq, WIDTH), BF16),
        scratch_shapes=[
            pltpu.VMEM((seq, HEAD_DIM), BF16),
            pltpu.VMEM((seq, HEAD_DIM), BF16),
            pltpu.VMEM((seq // kc, HEAD_DIM, kc), BF16),
            pltpu.VMEM((seq, tq), F32),
            pltpu.VMEM((seq, HEADS * tq), F32),
            pltpu.VMEM((HEAD_DIM, HEADS * tq), F32),
        ],
        compiler_params=pltpu.CompilerParams(
            dimension_semantics=("arbitrary", "arbitrary"), vmem_limit_bytes=VMEM_LIMIT),
        name="dsa",
    )(zd, zd, zd, zd, cos_t, sin_t, cos_t, sin_t)


def _merge_ffn_kernel(x_ref, ya_ref, yb_ref, zg_ref, wba_ref, wbb_ref, wout_ref, gffn_ref,
                      wup_ref, wdown_ref, gfin_ref, o_ref, *, final, hidden_chunk):
    d = x_ref.shape[1]
    pa = _dot(ya_ref[...], wba_ref[...])
    pb = _dot(yb_ref[...], wbb_ref[...])
    merged = _sigmoid(zg_ref[:, 0:d]) * pa + _sigmoid(zg_ref[:, d:2 * d]) * pb
    x1 = x_ref[...] + _dot(merged.astype(BF16), wout_ref[...])
    h2 = _rms(x1, gffn_ref[...]).astype(BF16)
    x2 = x1
    hidden = wup_ref.shape[1]
    for c in range(hidden // hidden_chunk):
        sl = slice(c * hidden_chunk, (c + 1) * hidden_chunk)
        u = jnp.maximum(_dot(h2, wup_ref[:, sl]), 0.0)
        x2 = x2 + _dot((u * u).astype(BF16), wdown_ref[sl, :])
    o_ref[...] = _rms(x2, gfin_ref[...]) if final else x2


def _merge_ffn(x2, ya, yb, zg, wba, wbb, wout, gffn, wup, wdown, gfin, tm, final):
    n, d = x2.shape
    const = lambda i: (0, 0)
    row = lambda i: (i, 0)
    resident = lambda arr: pl.BlockSpec(arr.shape, const, pipeline_mode=pl.Buffered(1))
    kernel = functools.partial(_merge_ffn_kernel, final=final,
                               hidden_chunk=min(1024, wup.shape[1]))
    return pl.pallas_call(
        kernel,
        grid=(n // tm,),
        in_specs=[
            pl.BlockSpec((tm, d), row),
            pl.BlockSpec((tm, WIDTH), row),
            pl.BlockSpec((tm, WIDTH), row),
            pl.BlockSpec((tm, 2 * d), row),
            resident(wba), resident(wbb), resident(wout),
            pl.BlockSpec((1, d), const),
            resident(wup), resident(wdown),
            pl.BlockSpec((1, d), const),
        ],
        out_specs=pl.BlockSpec((tm, d), row),
        out_shape=jax.ShapeDtypeStruct((n, d), F32),
        compiler_params=pltpu.CompilerParams(
            dimension_semantics=("arbitrary",), vmem_limit_bytes=VMEM_LIMIT),
        name="merge_ffn",
    )(x2, ya, yb, zg, wba, wbb, wout, gffn, wup, wdown, gfin)


def _rope_tables(seq):
    inv = 1.0 / (ROPE_THETA ** (jnp.arange(0, HEAD_DIM, 2, dtype=F32) / HEAD_DIM))
    ang = jnp.arange(seq, dtype=F32)[:, None] * inv[None, :]
    cos, sin = jnp.cos(ang), jnp.sin(ang)
    cos_t = jnp.concatenate([cos, cos] * (LANES // HEAD_DIM), axis=1)
    sin_t = jnp.concatenate([-sin, sin] * (LANES // HEAD_DIM), axis=1)
    return cos_t, sin_t


def _pad_cols(a, n):
    return jnp.pad(a, ((0, 0), (0, n - a.shape[1])))


def _pad_rows(a, n):
    return jnp.pad(a, ((0, n - a.shape[0]), (0, 0)))


def kernel(x, g_mix, w_in, mu_shift, decay_bias, w_decay_up, iclr_bias, w_iclr_up, w_gate_up,
           k_k, k_a, r_k, gn_w, gn_b, w_branch, w_out, g_ffn, w_ffn_up, w_ffn_down, g_final):
    batch, seq, d = x.shape
    depth = g_mix.shape[0]
    n = batch * seq
    tm = 256
    assert seq % min(Q_TILE, seq) == 0 and seq % min(KEY_CHUNK, seq) == 0
    assert seq % min(RWKV_BLOCK, seq) == 0
    assert min(RWKV_BLOCK, seq) % (RWKV_GROUPS * RWKV_CHUNK) == 0
    assert n % tm == 0
    cos_t, sin_t = _rope_tables(seq)
    row = lambda a: a.reshape(1, -1)
    xf = x.reshape(n, d)
    for l in range(depth):
        w = w_in[l]
        w_rwkv = _pad_cols(w[:, :RWKV_IN], RWKV_IN_PAD).astype(BF16)
        wd = w[:, RWKV_IN:RWKV_IN + DSA_IN]
        o_k, o_v, o_qi = WIDTH, WIDTH + HEAD_DIM, WIDTH + 2 * HEAD_DIM
        o_ki, o_wi = o_qi + WIDTH, o_qi + WIDTH + HEAD_DIM
        w_dsa = _pad_cols(jnp.concatenate(
            [wd[:, :WIDTH], wd[:, o_qi:o_ki], wd[:, o_k:o_v], wd[:, o_v:o_qi],
             wd[:, o_ki:o_wi], wd[:, o_wi:]], axis=1), DSA_IN_PAD).astype(BF16)
        w_gate = w[:, RWKV_IN + DSA_IN:].astype(BF16)
        zr, zd, zg = _inproj(xf, row(g_mix[l]), w_rwkv, w_dsa, w_gate, tm)

        wdec = _pad_rows(w_decay_up[l], LANES).astype(BF16)
        wicl = jnp.pad(w_iclr_up[l], ((DECAY_LORA, 0), (0, 0))).astype(BF16)
        wgate = _pad_rows(w_gate_up[l], GATE_LORA_PAD).astype(BF16)
        mu = _pad_cols(row(mu_shift[l]), RWKV_IN_PAD)
        ya = _rwkv(zr, batch, seq, mu, row(decay_bias[l]), wdec, row(iclr_bias[l]), wicl,
                   wgate, row(k_k[l]), row(k_a[l]), row(r_k[l]), row(gn_w[l]), row(gn_b[l]))
        yb = _dsa(zd, batch, seq, cos_t, sin_t)

        xf = _merge_ffn(xf, ya, yb, zg, w_branch[l, 0].astype(BF16),
                        w_branch[l, 1].astype(BF16), w_out[l].astype(BF16), row(g_ffn[l]),
                        w_ffn_up[l].astype(BF16), w_ffn_down[l].astype(BF16), row(g_final),
                        tm, final=(l == depth - 1))
    return xf.reshape(batch, seq, d)
```

```python
import functools

import numpy as np
import jax
import jax.numpy as jnp
from jax import lax
from jax.experimental import pallas as pl
from jax.experimental.pallas import tpu as pltpu

F32 = jnp.float32
BF16 = jnp.bfloat16

HEADS = 8
HEAD_DIM = 64
PAIRS = HEADS // 2
WIDTH = HEADS * HEAD_DIM
DECAY_LORA = 64
ICLR_LORA = 64
GATE_LORA = 160
GATE_LORA_PAD = 256
RWKV_IN = 3 * WIDTH + DECAY_LORA + ICLR_LORA + GATE_LORA
RWKV_IN_PAD = 3 * WIDTH + DECAY_LORA + ICLR_LORA + GATE_LORA_PAD
DSA_IN = WIDTH + 2 * HEAD_DIM + WIDTH + HEAD_DIM + HEADS
DSA_IN_PAD = 2 * WIDTH + 3 * 128
TOPK_MAX = 256
Q_TILE = 256
ROPE_THETA = 10000.0
NORM_EPS = 1e-6
RWKV_GN_EPS = 64e-5
L2_EPS = 1e-12

LANES = 128
SUBLANES = 8
MXU_TILE = 256
VMEM_LIMIT = 56 * 1024 * 1024

RWKV_CHUNK = 64
RWKV_BLOCK = 256
RWKV_GROUPS = 1
KEY_CHUNK = 256
DSA_CHUNKS_PER_TRIP = 4
SEARCH_UNROLL = 4
SEARCH_ITERS_MAX = 320
COUNT_CHAINS = 4
MASK_BIAS = -1e30


def _dot(a, b):
    return jnp.dot(a, b, preferred_element_type=F32)


def _dot_nt(a, b):
    return lax.dot_general(a, b, (((1,), (1,)), ((), ())), preferred_element_type=F32)


def _dot_tn(a, b):
    return lax.dot_general(a, b, (((0,), (0,)), ((), ())), preferred_element_type=F32)


def _split2(x):
    hi = x.astype(BF16)
    lo = (x - hi.astype(F32)).astype(BF16)
    return hi, lo


def _split3(x):
    h1 = x.astype(BF16)
    r1 = x - h1.astype(F32)
    h2 = r1.astype(BF16)
    h3 = (r1 - h2.astype(F32)).astype(BF16)
    return h1, h2, h3


def _rms(x, g):
    ms = jnp.mean(x * x, axis=-1, keepdims=True)
    return x * lax.rsqrt(ms + NORM_EPS) * g


def _sigmoid(x):
    return 1.0 / (1.0 + jnp.exp(-x))


def _swap_halves(x):
    cols = [pltpu.roll(x[:, g * LANES:(g + 1) * LANES], HEAD_DIM, 1)
            for g in range(x.shape[1] // LANES)]
    return jnp.concatenate(cols, axis=1)


def _inproj_kernel(x_ref, g_ref, wr_ref, wd_ref, wg_ref, zr_ref, zd_ref, zg_ref):
    h = _rms(x_ref[...], g_ref[...]).astype(BF16)
    zr_ref[...] = _dot(h, wr_ref[...])
    zd_ref[...] = _dot(h, wd_ref[...])
    zg_ref[...] = _dot(h, wg_ref[...])


def _inproj(x2, g, wr, wd, wg, tm):
    n, d = x2.shape
    const = lambda i: (0, 0)
    row = lambda i: (i, 0)
    return pl.pallas_call(
        _inproj_kernel,
        grid=(n // tm,),
        in_specs=[
            pl.BlockSpec((tm, d), row),
            pl.BlockSpec((1, d), const),
            pl.BlockSpec(wr.shape, const),
            pl.BlockSpec(wd.shape, const),
            pl.BlockSpec(wg.shape, const),
        ],
        out_specs=[
            pl.BlockSpec((tm, wr.shape[1]), row),
            pl.BlockSpec((tm, wd.shape[1]), row),
            pl.BlockSpec((tm, wg.shape[1]), row),
        ],
        out_shape=[
            jax.ShapeDtypeStruct((n, wr.shape[1]), F32),
            jax.ShapeDtypeStruct((n, wd.shape[1]), F32),
            jax.ShapeDtypeStruct((n, wg.shape[1]), F32),
        ],
        compiler_params=pltpu.CompilerParams(
            dimension_semantics=("arbitrary",), vmem_limit_bytes=VMEM_LIMIT),
        name="inproj",
    )(x2, g, wr, wd, wg)


def _rwkv_kernel(z_ref, mu_ref, dbias_ref, wdec_ref, ibias_ref, wicl_ref, wgate_ref,
                 kk_ref, ka_ref, rk_ref, gnw_ref, gnb_ref, tri_ref, bd_ref,
                 y_ref,
                 state_ref, carry_ref, qa_ref, bk_ref, bkh_ref, vsw_ref, pc_ref, yacc_ref):
    tb = z_ref.shape[0]
    c_len = RWKV_CHUNK
    n_chunks = tb // c_len

    @pl.when(pl.program_id(1) == 0)
    def _():
        state_ref[...] = jnp.zeros_like(state_ref)
        carry_ref[...] = jnp.zeros_like(carry_ref)

    n_groups = RWKV_GROUPS
    per_group = n_chunks // n_groups
    g_rows = per_group * c_len
    bd = bd_ref[...]
    bd_w = bd.shape[0]

    def headsum(t):
        tb16 = t.astype(BF16)
        return jnp.concatenate(
            [_dot(tb16[:, i:i + bd_w], bd) for i in range(0, WIDTH, bd_w)], axis=1)

    def prologue_parts(g, t):
        r0 = g * g_rows

        def part_a():
            z = z_ref[r0:r0 + g_rows, :]
            before = carry_ref[0:1, :] if g == 0 else z_ref[r0 - 1:r0, :]
            first_row = lax.broadcasted_iota(jnp.int32, (g_rows, 1), 0) == 0
            prev = jnp.where(first_row, before, pltpu.roll(z, 1, 0))
            zs = z + (prev - z) * mu_ref[...]
            t["r"] = zs[:, 0:WIDTH]
            t["k"] = zs[:, WIDTH:2 * WIDTH]
            t["v"] = zs[:, 2 * WIDTH:3 * WIDTH]
            lora_in = zs[:, 3 * WIDTH:3 * WIDTH + LANES]
            gate_in = zs[:, 3 * WIDTH + LANES:]
            dec_pre = dbias_ref[...] + _dot(jnp.tanh(lora_in).astype(BF16), wdec_ref[...])
            neg = -dec_pre
            softplus = jnp.maximum(neg, 0.0) + jnp.log1p(jnp.exp(-jnp.abs(neg)))
            w_log = -softplus - 0.5
            t["logw"] = -jnp.exp(w_log)
            t["iclr"] = _sigmoid(ibias_ref[...] + _dot(lora_in.astype(BF16), wicl_ref[...]))
            t["gate"] = _dot(_sigmoid(gate_in).astype(BF16), wgate_ref[...])
            vsw_ref[r0:r0 + g_rows, :] = _swap_halves(t["v"]).astype(BF16)

        def part_b():
            k, iclr, logw = t["k"], t["iclr"], t["logw"]
            kk = k * kk_ref[...]
            kk = kk / jnp.maximum(jnp.sqrt(headsum(kk * kk)), L2_EPS)
            t["k2"] = k * (1.0 + (iclr - 1.0) * ka_ref[...])
            t["a"] = -kk
            t["b"] = kk * iclr
            logw_hi, logw_lo = _split2(logw)
            tri = tri_ref[0:g_rows, 0:g_rows]
            t["cum"] = _dot(tri, logw_hi) + _dot(tri, logw_lo)

        def part_c():
            cum, logw, r, k2, a, b = t["cum"], t["logw"], t["r"], t["k2"], t["a"], t["b"]
            ends = [cum[(i + 1) * c_len - 1:(i + 1) * c_len, :] for i in range(per_group)]
            cum_end = jnp.concatenate([jnp.broadcast_to(e, (c_len, WIDTH)) for e in ends], axis=0)
            e_neg = jnp.exp(-cum)
            e_tail = jnp.exp(cum_end - cum)
            at = (a * jnp.exp(cum - logw)).astype(BF16)
            rt = (r * jnp.exp(cum)).astype(BF16)
            bt = (b * e_neg).astype(BF16)
            kt = (k2 * e_neg).astype(BF16)
            bh = (b * e_tail).astype(BF16)
            kh = (k2 * e_tail).astype(BF16)
            for i in range(per_group):
                c = g * per_group + i
                src = slice(i * c_len, (i + 1) * c_len)
                top = slice(2 * c * c_len, (2 * c + 1) * c_len)
                bot = slice((2 * c + 1) * c_len, (2 * c + 2) * c_len)
                qa_ref[top, :] = at[src]
                qa_ref[bot, :] = rt[src]
                bk_ref[top, :] = bt[src]
                bk_ref[bot, :] = kt[src]
                bkh_ref[top, :] = bh[src]
                bkh_ref[bot, :] = kh[src]
                pc_ref[c * SUBLANES:c * SUBLANES + 1, :] = jnp.exp(ends[i])
            t["rkr"] = r * k2 * rk_ref[...]

        return [part_a, part_b, part_c]

    ri = lax.broadcasted_iota(jnp.int32, (c_len, 2 * c_len), 0)
    ci = lax.broadcasted_iota(jnp.int32, (c_len, 2 * c_len), 1) % c_len
    strict = ci < ri
    incl = ci <= ri
    lane = lax.broadcasted_iota(jnp.int32, (1, LANES), 1)
    half = [lane < HEAD_DIM, lane >= HEAD_DIM]
    zero_bf = jnp.zeros((), BF16)
    zero_top = jnp.zeros((c_len, LANES), BF16)
    hs = range(HEADS)
    lanes = [slice((h // 2) * LANES, (h // 2 + 1) * LANES) for h in hs]
    own = [half[h % 2] for h in hs]
    other = [half[1 - h % 2] for h in hs]
    steps = int(np.log2(c_len))

    def independent(g, hooks):
        cs = [g * per_group + i for i in range(per_group)]
        items = [(c, h) for c in cs for h in hs]
        rows2 = {c: slice(2 * c * c_len, (2 * c + 2) * c_len) for c in cs}
        rows = {c: slice(c * c_len, (c + 1) * c_len) for c in cs}
        qa = {(c, h): jnp.where(own[h], qa_ref[rows2[c], lanes[h]], zero_bf) for c, h in items}
        ab = {(c, h): _dot_nt(qa[c, h], bk_ref[rows2[c], lanes[h]]) for c, h in items}
        mtop = {k: jnp.where(strict, ab[k][:c_len], 0.0) for k in items}
        mbot = {k: jnp.where(incl, ab[k][c_len:], 0.0).astype(BF16) for k in items}
        vp = {(c, h): jnp.where(other[h], vsw_ref[rows[c], lanes[h]], zero_bf) for c, h in items}
        lkv = {k: _dot(mtop[k].astype(BF16), jnp.concatenate([zero_top, vp[k]], axis=0))
               for k in items}
        x = {k: qa[k][:c_len].astype(F32) + lkv[k] for k in items}
        pw = {k: mtop[k][:, :c_len].astype(BF16) for k in items}
        for j in range(steps):
            x = {k: x[k] + _dot(pw[k], x[k].astype(BF16)) for k in items}
            if j + 1 < steps:
                pw = {k: _dot(pw[k], pw[k]).astype(BF16) for k in items}
            for hook in hooks[j]:
                hook()
        rhs = {k: jnp.concatenate([x[k].astype(BF16), vp[k]], axis=0) for k in items}
        gy = {k: _dot(mbot[k], rhs[k]) for k in items}
        mn = {(c, h): _dot_tn(rhs[c, h],
                              jnp.where(own[h], bkh_ref[rows2[c], lanes[h]], zero_bf))
              for c, h in items}
        gr = {k: qa[k][c_len:].astype(F32) + gy[k] for k in items}
        return gy, mn, gr

    st = [state_ref[p] for p in range(PAIRS)]

    def dependent(c, res):
        gy, mn, gr = res
        sm = [jnp.where(own[h], st[h // 2], 0.0).astype(BF16) for h in hs]
        upd = [_dot(sm[h], mn[c, h].astype(BF16))
               + mn[c, h][(1 - h % 2) * c_len:(2 - h % 2) * c_len] for h in hs]
        ss = [jnp.concatenate([st[p].astype(BF16)] * 2, axis=0) for p in range(PAIRS)]
        gm = [jnp.where(own[h], gr[c, h], 0.0).astype(BF16) for h in hs]
        yfull = [gy[c, h] + _dot_nt(gm[h], ss[h // 2]) for h in hs]
        for p in range(PAIRS):
            pl_lanes = slice(p * LANES, (p + 1) * LANES)
            pc = pc_ref[c * SUBLANES:c * SUBLANES + 1, pl_lanes]
            st[p] = st[p] * pc + upd[2 * p] + upd[2 * p + 1]
            yacc_ref[c * c_len:(c + 1) * c_len, pl_lanes] = jnp.where(
                half[0], yfull[2 * p + 1], yfull[2 * p])

    tails = [dict() for _ in range(n_groups)]
    for part in prologue_parts(0, tails[0]):
        part()
    prev_res = None
    for g in range(n_groups):
        hooks = {j: [] for j in range(steps)}
        if g + 1 < n_groups:
            for n, part in enumerate(prologue_parts(g + 1, tails[g + 1])):
                hooks[min(steps - 1, 2 * n)].append(part)
        if prev_res is not None:
            for i in range(per_group):
                at_step = min(steps - 1, (i + 1) * steps // (per_group + 1))
                hooks[at_step].append(
                    functools.partial(dependent, (g - 1) * per_group + i, prev_res))
        prev_res = independent(g, hooks)
    for i in range(per_group):
        dependent((n_groups - 1) * per_group + i, prev_res)
    for p in range(PAIRS):
        state_ref[p] = st[p]
    carry_ref[0:1, :] = z_ref[tb - 1:tb, :]

    rkr = jnp.concatenate([t["rkr"] for t in tails], axis=0)
    v = jnp.concatenate([t["v"] for t in tails], axis=0)
    gate = jnp.concatenate([t["gate"] for t in tails], axis=0)
    y = _swap_halves(yacc_ref[...])
    inv_n = 1.0 / HEAD_DIM
    mean = headsum(y) * inv_n
    yc = y - mean
    var = headsum(yc * yc) * inv_n
    yn = yc * lax.rsqrt(var + RWKV_GN_EPS) * gnw_ref[...] + gnb_ref[...]
    bonus = headsum(rkr) * v
    y_ref[...] = ((yn + bonus) * gate).astype(y_ref.dtype)


def _rwkv(zr, batch, seq, mu, dbias, wdec, ibias, wicl, wgate, kk, ka, rk, gnw, gnb):
    tb = min(RWKV_BLOCK, seq)
    n_t = seq // tb
    idx = np.arange(tb)
    same_chunk = (idx[:, None] // RWKV_CHUNK) == (idx[None, :] // RWKV_CHUNK)
    tri = jnp.asarray(same_chunk & (idx[None, :] <= idx[:, None]), BF16)
    hid = np.arange(MXU_TILE) // HEAD_DIM
    bd = jnp.asarray(hid[:, None] == hid[None, :], BF16)
    const = lambda b, i: (0, 0)
    row = lambda b, i: (b * n_t + i, 0)
    vec = lambda n: pl.BlockSpec((1, n), const)
    full = lambda arr: pl.BlockSpec(arr.shape, const)
    return pl.pallas_call(
        _rwkv_kernel,
        grid=(batch, n_t),
        in_specs=[
            pl.BlockSpec((tb, RWKV_IN_PAD), row),
            vec(RWKV_IN_PAD), vec(WIDTH), full(wdec), vec(WIDTH), full(wicl), full(wgate),
            vec(WIDTH), vec(WIDTH), vec(WIDTH), vec(WIDTH), vec(WIDTH),
            full(tri), full(bd),
        ],
        out_specs=pl.BlockSpec((tb, WIDTH), row),
        out_shape=jax.ShapeDtypeStruct((batch * seq, WIDTH), BF16),
        scratch_shapes=[
            pltpu.VMEM((PAIRS, HEAD_DIM, LANES), F32),
            pltpu.VMEM((SUBLANES, RWKV_IN_PAD), F32),
            pltpu.VMEM((2 * tb, WIDTH), BF16),
            pltpu.VMEM((2 * tb, WIDTH), BF16),
            pltpu.VMEM((2 * tb, WIDTH), BF16),
            pltpu.VMEM((tb, WIDTH), BF16),
            pltpu.VMEM((SUBLANES * (tb // RWKV_CHUNK), WIDTH), F32),
            pltpu.VMEM((tb, WIDTH), F32),
        ],
        compiler_params=pltpu.CompilerParams(
            dimension_semantics=("arbitrary", "arbitrary"), vmem_limit_bytes=VMEM_LIMIT),
        name="rwkv7",
    )(zr, mu, dbias, wdec, ibias, wicl, wgate, kk, ka, rk, gnw, gnb, tri, bd)


def _rope(x, cos, sin_signed):
    out = []
    for g in range(x.shape[1] // LANES):
        xg = x[:, g * LANES:(g + 1) * LANES]
        out.append(xg * cos + pltpu.roll(xg, LANES // 2, 1) * sin_signed)
    return out[0] if len(out) == 1 else jnp.concatenate(out, axis=1)


def _fold(x, fn):
    rows, n = x.shape
    parts = x.reshape(rows // SUBLANES, SUBLANES, n)
    level = [parts[i] for i in range(rows // SUBLANES)]
    while len(level) > 1:
        nxt = [fn(level[i], level[i + 1]) for i in range(0, len(level) - 1, 2)]
        if len(level) % 2:
            nxt.append(level[-1])
        level = nxt
    return level[0]


def _pair_layout():
    half = HEAD_DIM // 2
    def pair_cols(base):
        cols = []
        for p in range(PAIRS):
            a, b = base + 2 * p * HEAD_DIM, base + (2 * p + 1) * HEAD_DIM
            cols += [*range(a, a + half), *range(b, b + half),
                     *range(a + half, a + HEAD_DIM), *range(b + half, b + HEAD_DIM)]
        return cols
    def shared_cols(base):
        x1, x2 = list(range(base, base + half)), list(range(base + half, base + HEAD_DIM))
        return x1 + x1 + x2 + x2
    o_k, o_v, o_qi = WIDTH, WIDTH + HEAD_DIM, WIDTH + 2 * HEAD_DIM
    o_ki, o_wi = o_qi + WIDTH, o_qi + WIDTH + HEAD_DIM
    last = [*range(o_v, o_v + HEAD_DIM), *range(o_wi, o_wi + HEADS)]
    last += [-1] * (LANES - len(last))
    order = pair_cols(0) + pair_cols(o_qi) + shared_cols(o_k) + shared_cols(o_ki) + last
    assert len(order) == DSA_IN_PAD
    return np.asarray(order, np.int32)


def _heads_to_rows(x):
    lane = lax.broadcasted_iota(jnp.int32, (1, LANES), 1)
    second = (lane // (HEAD_DIM // 2)) % 2 == 1
    rows = []
    for h in range(HEADS):
        pair = x[:, (h // 2) * LANES:(h // 2 + 1) * LANES]
        rows.append(jnp.where(second == (h % 2 == 1), pair, 0.0))
    return jnp.concatenate(rows, axis=0)


def _dsa_kernel(qq_ref, k_ref, ki_ref, vw_ref, wq_ref, cosq_ref, sinq_ref, cosk_ref, sink_ref,
                o_ref,
                k_s, ki_s, vt_s, sc_s, s_all, acc_s, *, topk, idx_scale):
    j = pl.program_id(1)
    tq = qq_ref.shape[0]
    seq = k_ref.shape[0]
    kc = min(KEY_CHUNK, seq)
    n_ch = (j * tq + tq + kc - 1) // kc
    inf = jnp.float32(jnp.inf)

    @pl.when(j == 0)
    def _():
        k_s[...] = _rope(k_ref[...], cosk_ref[...], sink_ref[...]).astype(BF16)
        ki_s[...] = _rope(ki_ref[...], cosk_ref[...], sink_ref[...]).astype(BF16)
        vwt = vw_ref[...].T
        for c in range(seq // kc):
            vt_s[c] = vwt[:HEAD_DIM, c * kc:(c + 1) * kc].astype(BF16)

    cosq = cosq_ref[...]
    sinq = sinq_ref[...]
    q_all = _heads_to_rows(_rope(qq_ref[:, 0:WIDTH], cosq, sinq)
                           * (HEAD_DIM ** -0.5)).astype(BF16)
    qi_all = _heads_to_rows(_rope(qq_ref[:, WIDTH:2 * WIDTH], cosq, sinq)).astype(BF16)

    sel_r = lax.broadcasted_iota(jnp.int32, (2 * SUBLANES, LANES), 0)
    sel_c = lax.broadcasted_iota(jnp.int32, (2 * SUBLANES, LANES), 1)
    sel = (sel_c == sel_r + HEAD_DIM).astype(BF16)
    w1, w2, w3 = _split3(wq_ref[...])
    w_t = (_dot_nt(sel, w1) + (_dot_nt(sel, w2) + _dot_nt(sel, w3))) * idx_scale

    qpos = j * tq + lax.broadcasted_iota(jnp.int32, (1, tq), 1)
    key_in_chunk = lax.broadcasted_iota(jnp.int32, (kc, tq), 0)

    def chunk_rows(c):
        return pl.ds(pl.multiple_of(c * kc, kc), kc)

    def paired_loop(step, carry):
        w = DSA_CHUNKS_PER_TRIP
        carry = lax.fori_loop(
            0, n_ch // w, lambda t, cr: step([w * t + i for i in range(w)], cr), carry)
        width = w // 2
        while width >= 1:
            base = (n_ch // (2 * width)) * (2 * width)
            carry = lax.fori_loop(
                0, (n_ch // width) % 2,
                lambda _, cr, base=base, width=width: step([base + i for i in range(width)], cr),
                carry)
            width //= 2
        return carry

    def score_step(cs, carry):
        rmax, rmin = carry
        lg = [[_dot_nt(ki_s[chunk_rows(c), :], qi_all[h * tq:(h + 1) * tq])
               for h in range(HEADS)] for c in cs]
        for i, c in enumerate(cs):
            acc = w_t[0:1] * jnp.maximum(lg[i][0], 0.0)
            for h in range(1, HEADS):
                acc = acc + w_t[h:h + 1] * jnp.maximum(lg[i][h], 0.0)
            causal = (c * kc + key_in_chunk) <= qpos
            lo_fill = jnp.where(causal, acc, -inf)
            sc_s[chunk_rows(c), :] = lo_fill
            rmax = jnp.maximum(rmax, _fold(lo_fill, jnp.maximum))
            rmin = jnp.minimum(rmin, _fold(jnp.where(causal, acc, inf), jnp.minimum))
        return rmax, rmin

    rmax, rmin = paired_loop(
        score_step,
        (jnp.full((SUBLANES, tq), -inf, F32), jnp.full((SUBLANES, tq), inf, F32)))
    rmax = jnp.max(rmax, axis=0, keepdims=True)
    rmin = jnp.min(rmin, axis=0, keepdims=True)

    def count(cmp, x):
        xb = jnp.broadcast_to(x, (SUBLANES, tq))

        def body(c, accs):
            parts = sc_s[chunk_rows(c), :].reshape(kc // SUBLANES, SUBLANES, tq)
            accs = list(accs)
            for i in range(kc // SUBLANES):
                a = accs[i % COUNT_CHAINS]
                accs[i % COUNT_CHAINS] = jnp.where(cmp(parts[i], xb), a + 1.0, a)
            return tuple(accs)

        accs = lax.fori_loop(0, n_ch, body,
                             tuple(jnp.zeros((SUBLANES, tq), F32) for _ in range(COUNT_CHAINS)))
        return jnp.sum(_fold(jnp.concatenate(accs, axis=0), jnp.add), axis=0, keepdims=True)

    kf = jnp.float32(topk)
    n_causal = (qpos + 1).astype(F32)
    all_keys = n_causal < kf
    ge = lambda s, x: s >= x
    gt = lambda s, x: s > x
    zero = jnp.zeros((1, tq), F32)
    c_max = count(ge, rmax)
    c_ge0 = count(ge, zero)
    c_gt0 = count(gt, zero)
    top_tied = c_max >= kf
    zero_thr = (c_gt0 < kf) & (c_ge0 >= kf)
    above0 = c_gt0 >= kf
    lo = jnp.where(top_tied, rmax, jnp.where(zero_thr | above0, zero, rmin))
    c_lo = jnp.where(top_tied, c_max, jnp.where(zero_thr | above0, c_ge0, n_causal))
    hi = jnp.where(top_tied | above0, rmax, zero)
    done = all_keys | top_tied | zero_thr | (c_lo == kf) | (lo == hi)

    def search_cond(st):
        _, _, _, done_f, it = st
        return jnp.logical_and(jnp.min(done_f) < 0.5, it < SEARCH_ITERS_MAX)

    def search_body(st):
        lo, hi, c_lo, done_f, it = st
        for _ in range(SEARCH_UNROLL):
            active = done_f < 0.5
            mid = lo * 0.5 + hi * 0.5
            c = count(ge, mid)
            enough = c >= kf
            stuck = (mid == lo) | (mid == hi)
            lo = jnp.where(active & enough, mid, lo)
            hi = jnp.where(active & jnp.logical_not(enough), mid, hi)
            c_lo = jnp.where(active & enough, c, c_lo)
            done_f = jnp.where(active & ((c_lo == kf) | stuck), jnp.float32(1.0), done_f)
        return lo, hi, c_lo, done_f, it + SEARCH_UNROLL

    lo, hi, c_lo, _, _ = lax.while_loop(
        search_cond, search_body, (lo, hi, c_lo, done.astype(F32), jnp.int32(0)))
    lo = jnp.where(all_keys, -inf, lo)
    exact_k = all_keys | (c_lo == kf)
    no_ties = jnp.min(exact_k.astype(F32)) > 0.5

    @pl.when(no_ties)
    def _():
        def body(c, carry):
            s = sc_s[chunk_rows(c), :]
            causal = (c * kc + key_in_chunk) <= qpos
            sc_s[chunk_rows(c), :] = jnp.where((s >= lo) & causal, 0.0, MASK_BIAS)
            return carry
        lax.fori_loop(0, n_ch, body, 0)

    @pl.when(jnp.logical_not(no_ties))
    def _():
        def snap_body(c, acc):
            s = sc_s[chunk_rows(c), :]
            return jnp.minimum(acc, _fold(jnp.where(s >= lo, s, inf), jnp.minimum))
        thr = jnp.min(lax.fori_loop(0, n_ch, snap_body, jnp.full((SUBLANES, tq), inf, F32)),
                      axis=0, keepdims=True)
        need = kf - count(gt, thr)
        ki_ = lax.broadcasted_iota(jnp.int32, (kc, kc), 0)
        kj_ = lax.broadcasted_iota(jnp.int32, (kc, kc), 1)
        earlier = (kj_ < ki_).astype(BF16)
        ones = jnp.ones((2 * SUBLANES, kc), BF16)

        def body(c, run):
            s = sc_s[chunk_rows(c), :]
            causal = (c * kc + key_in_chunk) <= qpos
            eq = (s == thr).astype(BF16)
            rank = run + _dot(earlier, eq)
            take = (s > thr) | ((s == thr) & (rank < need))
            sc_s[chunk_rows(c), :] = jnp.where(take & causal, 0.0, MASK_BIAS)
            return run + _dot(ones, eq)[0:1]
        lax.fori_loop(0, n_ch, body, jnp.zeros((1, tq), F32))

    def logits_step(cs, m_acc):
        s = [[_dot_nt(k_s[chunk_rows(c), :], q_all[h * tq:(h + 1) * tq])
              for h in range(HEADS)] for c in cs]
        for i, c in enumerate(cs):
            bias = sc_s[chunk_rows(c), :]
            out = []
            for h in range(HEADS):
                sh = s[i][h] + bias
                s_all[chunk_rows(c), h * tq:(h + 1) * tq] = sh
                out.append(_fold(sh, jnp.maximum))
            m_acc = jnp.maximum(m_acc, jnp.concatenate(out, axis=1))
        return m_acc

    m = jnp.max(paired_loop(logits_step, jnp.full((SUBLANES, HEADS * tq), MASK_BIAS, F32)),
                axis=0, keepdims=True)

    def pv_step(cs, l_acc):
        out = []
        for h in range(HEADS):
            cols = slice(h * tq, (h + 1) * tq)
            p = [jnp.exp(s_all[chunk_rows(c), cols] - m[:, cols]) for c in cs]
            fold = _fold(p[0], jnp.add)
            upd = _dot(vt_s[cs[0]], p[0].astype(BF16))
            for i in range(1, len(cs)):
                fold = fold + _fold(p[i], jnp.add)
                upd = upd + _dot(vt_s[cs[i]], p[i].astype(BF16))
            out.append(fold)
            acc_s[:, cols] += upd
        return l_acc + jnp.concatenate(out, axis=1)

    acc_s[...] = jnp.zeros_like(acc_s)
    l_acc = paired_loop(pv_step, jnp.zeros((SUBLANES, HEADS * tq), F32))
    out_t = acc_s[...] / jnp.sum(l_acc, axis=0, keepdims=True)
    for p in range(PAIRS):
        pair = jnp.concatenate([out_t[:, 2 * p * tq:(2 * p + 1) * tq],
                                out_t[:, (2 * p + 1) * tq:(2 * p + 2) * tq]], axis=0)
        o_ref[:, p * LANES:(p + 1) * LANES] = pair.T.astype(o_ref.dtype)


def _dsa(zd, batch, seq, cos_t, sin_t):
    tq = min(Q_TILE, seq)
    nb = seq // tq
    topk = min(TOPK_MAX, seq // 4)
    kc = min(KEY_CHUNK, seq)
    idx_scale = HEADS ** -0.5 * HEAD_DIM ** -0.5
    kernel = functools.partial(_dsa_kernel, topk=topk, idx_scale=idx_scale)
    qcols = 2 * WIDTH // LANES
    return pl.pallas_call(
        kernel,
        grid=(batch, nb),
        in_specs=[
            pl.BlockSpec((tq, 2 * WIDTH), lambda b, j: (b * nb + j, 0)),
            pl.BlockSpec((seq, LANES), lambda b, j: (b, qcols)),
            pl.BlockSpec((seq, LANES), lambda b, j: (b, qcols + 1)),
            pl.BlockSpec((seq, LANES), lambda b, j: (b, qcols + 2)),
            pl.BlockSpec((tq, LANES), lambda b, j: (b * nb + j, qcols + 2)),
            pl.BlockSpec((tq, LANES), lambda b, j: (j, 0)),
            pl.BlockSpec((tq, LANES), lambda b, j: (j, 0)),
            pl.BlockSpec((seq, LANES), lambda b, j: (0, 0)),
            pl.BlockSpec((seq, LANES), lambda b, j: (0, 0)),
        ],
        out_specs=pl.BlockSpec((tq, WIDTH), lambda b, j: (b * nb + j, 0)),
        out_shape=jax.ShapeDtypeStruct((batch * seq, WIDTH), BF16),
        scratch_shapes=[
            pltpu.VMEM((seq, LANES), BF16),
            pltpu.VMEM((seq, LANES), BF16),
            pltpu.VMEM((seq // kc, HEAD_DIM, kc), BF16),
            pltpu.VMEM((seq, tq), F32),
            pltpu.VMEM((seq, HEADS * tq), F32),
            pltpu.VMEM((HEAD_DIM, HEADS * tq), F32),
        ],
        compiler_params=pltpu.CompilerParams(
            dimension_semantics=("arbitrary", "arbitrary"), vmem_limit_bytes=VMEM_LIMIT),
        name="dsa",
    )(zd, zd, zd, zd, zd, cos_t, sin_t, cos_t, sin_t)


def _merge_ffn_kernel(x_ref, ya_ref, yb_ref, zg_ref, wba_ref, wbb_ref, wout_ref, gffn_ref,
                      wup_ref, wdown_ref, gfin_ref, o_ref, *, final, hidden_chunk):
    d = x_ref.shape[1]
    pa = _dot(ya_ref[...], wba_ref[...])
    pb = _dot(yb_ref[...], wbb_ref[...])
    merged = _sigmoid(zg_ref[:, 0:d]) * pa + _sigmoid(zg_ref[:, d:2 * d]) * pb
    x1 = x_ref[...] + _dot(merged.astype(BF16), wout_ref[...])
    h2 = _rms(x1, gffn_ref[...]).astype(BF16)
    x2 = x1
    hidden = wup_ref.shape[1]
    for c in range(hidden // hidden_chunk):
        sl = slice(c * hidden_chunk, (c + 1) * hidden_chunk)
        u = jnp.maximum(_dot(h2, wup_ref[:, sl]), 0.0)
        x2 = x2 + _dot((u * u).astype(BF16), wdown_ref[sl, :])
    o_ref[...] = _rms(x2, gfin_ref[...]) if final else x2


def _merge_ffn(x2, ya, yb, zg, wba, wbb, wout, gffn, wup, wdown, gfin, tm, final):
    n, d = x2.shape
    const = lambda i: (0, 0)
    row = lambda i: (i, 0)
    resident = lambda arr: pl.BlockSpec(arr.shape, const, pipeline_mode=pl.Buffered(1))
    kernel = functools.partial(_merge_ffn_kernel, final=final,
                               hidden_chunk=min(1024, wup.shape[1]))
    return pl.pallas_call(
        kernel,
        grid=(n // tm,),
        in_specs=[
            pl.BlockSpec((tm, d), row),
            pl.BlockSpec((tm, WIDTH), row),
            pl.BlockSpec((tm, WIDTH), row),
            pl.BlockSpec((tm, 2 * d), row),
            resident(wba), resident(wbb), resident(wout),
            pl.BlockSpec((1, d), const),
            resident(wup), resident(wdown),
            pl.BlockSpec((1, d), const),
        ],
        out_specs=pl.BlockSpec((tm, d), row),
        out_shape=jax.ShapeDtypeStruct((n, d), F32),
        compiler_params=pltpu.CompilerParams(
            dimension_semantics=("arbitrary",), vmem_limit_bytes=VMEM_LIMIT),
        name="merge_ffn",
    )(x2, ya, yb, zg, wba, wbb, wout, gffn, wup, wdown, gfin)


def _rope_tables(seq):
    inv = 1.0 / (ROPE_THETA ** (jnp.arange(0, HEAD_DIM, 2, dtype=F32) / HEAD_DIM))
    ang = jnp.arange(seq, dtype=F32)[:, None] * inv[None, :]
    cos, sin = jnp.cos(ang), jnp.sin(ang)
    cos_t = jnp.concatenate([cos, cos, cos, cos], axis=1)
    sin_t = jnp.concatenate([-sin, -sin, sin, sin], axis=1)
    return cos_t, sin_t


def _pad_cols(a, n):
    return jnp.pad(a, ((0, 0), (0, n - a.shape[1])))


def _pad_rows(a, n):
    return jnp.pad(a, ((0, n - a.shape[0]), (0, 0)))


def kernel(x, g_mix, w_in, mu_shift, decay_bias, w_decay_up, iclr_bias, w_iclr_up, w_gate_up,
           k_k, k_a, r_k, gn_w, gn_b, w_branch, w_out, g_ffn, w_ffn_up, w_ffn_down, g_final):
    batch, seq, d = x.shape
    depth = g_mix.shape[0]
    n = batch * seq
    tm = 512
    assert seq % min(Q_TILE, seq) == 0 and seq % min(KEY_CHUNK, seq) == 0
    assert seq % min(RWKV_BLOCK, seq) == 0
    assert min(RWKV_BLOCK, seq) % (RWKV_GROUPS * RWKV_CHUNK) == 0
    assert n % tm == 0
    cos_t, sin_t = _rope_tables(seq)
    row = lambda a: a.reshape(1, -1)
    xf = x.reshape(n, d)
    for l in range(depth):
        w = w_in[l]
        w_rwkv = _pad_cols(w[:, :RWKV_IN], RWKV_IN_PAD).astype(BF16)
        wd = _pad_cols(w[:, RWKV_IN:RWKV_IN + DSA_IN], DSA_IN + 1)
        order = _pair_layout()
        w_dsa = jnp.take(wd, jnp.asarray(np.where(order < 0, DSA_IN, order)),
                         axis=1).astype(BF16)
        w_gate = w[:, RWKV_IN + DSA_IN:].astype(BF16)
        zr, zd, zg = _inproj(xf, row(g_mix[l]), w_rwkv, w_dsa, w_gate, tm)

        wdec = _pad_rows(w_decay_up[l], LANES).astype(BF16)
        wicl = jnp.pad(w_iclr_up[l], ((DECAY_LORA, 0), (0, 0))).astype(BF16)
        wgate = _pad_rows(w_gate_up[l], GATE_LORA_PAD).astype(BF16)
        mu = _pad_cols(row(mu_shift[l]), RWKV_IN_PAD)
        ya = _rwkv(zr, batch, seq, mu, row(decay_bias[l]), wdec, row(iclr_bias[l]), wicl,
                   wgate, row(k_k[l]), row(k_a[l]), row(r_k[l]), row(gn_w[l]), row(gn_b[l]))
        yb = _dsa(zd, batch, seq, cos_t, sin_t)

        xf = _merge_ffn(xf, ya, yb, zg, w_branch[l, 0].astype(BF16),
                        w_branch[l, 1].astype(BF16), w_out[l].astype(BF16), row(g_ffn[l]),
                        w_ffn_up[l].astype(BF16), w_ffn_down[l].astype(BF16), row(g_final),
                        tm, final=(l == depth - 1))
    return xf.reshape(batch, seq, d)
```

```python
import functools

import numpy as np
import jax
import jax.numpy as jnp
from jax import lax
from jax.experimental import pallas as pl
from jax.experimental.pallas import tpu as pltpu

F32 = jnp.float32
BF16 = jnp.bfloat16

HEADS = 8
HEAD_DIM = 64
PAIRS = HEADS // 2
WIDTH = HEADS * HEAD_DIM
DECAY_LORA = 64
ICLR_LORA = 64
GATE_LORA = 160
GATE_LORA_PAD = 256
RWKV_IN = 3 * WIDTH + DECAY_LORA + ICLR_LORA + GATE_LORA
RWKV_IN_PAD = 3 * WIDTH + DECAY_LORA + ICLR_LORA + GATE_LORA_PAD
DSA_IN = WIDTH + 2 * HEAD_DIM + WIDTH + HEAD_DIM + HEADS
DSA_IN_PAD = 2 * WIDTH + 3 * 128
TOPK_MAX = 256
Q_TILE = 256
ROPE_THETA = 10000.0
NORM_EPS = 1e-6
RWKV_GN_EPS = 64e-5
L2_EPS = 1e-12

LANES = 128
SUBLANES = 8
MXU_TILE = 256
VMEM_LIMIT = 56 * 1024 * 1024

RWKV_CHUNK = 64
RWKV_BLOCK = 256
RWKV_GROUPS = 1
KEY_CHUNK = 256
DSA_CHUNKS_PER_TRIP = 4
SEARCH_UNROLL = 4
SEARCH_ITERS_MAX = 320
COUNT_CHAINS = 4
MASK_BIAS = -1e30


def _dot(a, b):
    return jnp.dot(a, b, preferred_element_type=F32)


def _dot_nt(a, b):
    return lax.dot_general(a, b, (((1,), (1,)), ((), ())), preferred_element_type=F32)


def _dot_tn(a, b):
    return lax.dot_general(a, b, (((0,), (0,)), ((), ())), preferred_element_type=F32)


def _split2(x):
    hi = x.astype(BF16)
    lo = (x - hi.astype(F32)).astype(BF16)
    return hi, lo


def _split3(x):
    h1 = x.astype(BF16)
    r1 = x - h1.astype(F32)
    h2 = r1.astype(BF16)
    h3 = (r1 - h2.astype(F32)).astype(BF16)
    return h1, h2, h3


def _rms(x, g):
    ms = jnp.mean(x * x, axis=-1, keepdims=True)
    return x * lax.rsqrt(ms + NORM_EPS) * g


def _sigmoid(x):
    return 1.0 / (1.0 + jnp.exp(-x))


def _swap_halves(x):
    cols = [pltpu.roll(x[:, g * LANES:(g + 1) * LANES], HEAD_DIM, 1)
            for g in range(x.shape[1] // LANES)]
    return jnp.concatenate(cols, axis=1)


def _inproj_kernel(x_ref, g_ref, wr_ref, wd_ref, wg_ref, zr_ref, zd_ref, zg_ref):
    h = _rms(x_ref[...], g_ref[...]).astype(BF16)
    zr_ref[...] = _dot(h, wr_ref[...])
    zd_ref[...] = _dot(h, wd_ref[...])
    zg_ref[...] = _dot(h, wg_ref[...])


def _inproj(x2, g, wr, wd, wg, tm):
    n, d = x2.shape
    const = lambda i: (0, 0)
    row = lambda i: (i, 0)
    return pl.pallas_call(
        _inproj_kernel,
        grid=(n // tm,),
        in_specs=[
            pl.BlockSpec((tm, d), row),
            pl.BlockSpec((1, d), const),
            pl.BlockSpec(wr.shape, const),
            pl.BlockSpec(wd.shape, const),
            pl.BlockSpec(wg.shape, const),
        ],
        out_specs=[
            pl.BlockSpec((tm, wr.shape[1]), row),
            pl.BlockSpec((tm, wd.shape[1]), row),
            pl.BlockSpec((tm, wg.shape[1]), row),
        ],
        out_shape=[
            jax.ShapeDtypeStruct((n, wr.shape[1]), F32),
            jax.ShapeDtypeStruct((n, wd.shape[1]), F32),
            jax.ShapeDtypeStruct((n, wg.shape[1]), F32),
        ],
        compiler_params=pltpu.CompilerParams(
            dimension_semantics=("arbitrary",), vmem_limit_bytes=VMEM_LIMIT),
        name="inproj",
    )(x2, g, wr, wd, wg)


def _rwkv_kernel(z_ref, mu_ref, dbias_ref, wdec_ref, ibias_ref, wicl_ref, wgate_ref,
                 kk_ref, ka_ref, rk_ref, gnw_ref, gnb_ref, tri_ref, bd_ref,
                 y_ref,
                 state_ref, carry_ref, qa_ref, bk_ref, bkh_ref, vsw_ref, pc_ref, yacc_ref):
    tb = z_ref.shape[0]
    c_len = RWKV_CHUNK
    n_chunks = tb // c_len

    @pl.when(pl.program_id(1) == 0)
    def _():
        state_ref[...] = jnp.zeros_like(state_ref)
        carry_ref[...] = jnp.zeros_like(carry_ref)

    n_groups = RWKV_GROUPS
    per_group = n_chunks // n_groups
    g_rows = per_group * c_len
    bd = bd_ref[...]
    bd_w = bd.shape[0]

    def headsum(t):
        tb16 = t.astype(BF16)
        return jnp.concatenate(
            [_dot(tb16[:, i:i + bd_w], bd) for i in range(0, WIDTH, bd_w)], axis=1)

    def prologue_parts(g, t):
        r0 = g * g_rows

        def part_a():
            z = z_ref[r0:r0 + g_rows, :]
            before = carry_ref[0:1, :] if g == 0 else z_ref[r0 - 1:r0, :]
            first_row = lax.broadcasted_iota(jnp.int32, (g_rows, 1), 0) == 0
            prev = jnp.where(first_row, before, pltpu.roll(z, 1, 0))
            zs = z + (prev - z) * mu_ref[...]
            t["r"] = zs[:, 0:WIDTH]
            t["k"] = zs[:, WIDTH:2 * WIDTH]
            t["v"] = zs[:, 2 * WIDTH:3 * WIDTH]
            lora_in = zs[:, 3 * WIDTH:3 * WIDTH + LANES]
            gate_in = zs[:, 3 * WIDTH + LANES:]
            dec_pre = dbias_ref[...] + _dot(jnp.tanh(lora_in).astype(BF16), wdec_ref[...])
            neg = -dec_pre
            softplus = jnp.maximum(neg, 0.0) + jnp.log(1.0 + jnp.exp(-jnp.abs(neg)))
            w_log = -softplus - 0.5
            t["logw"] = -jnp.exp(w_log)
            t["iclr"] = _sigmoid(ibias_ref[...] + _dot(lora_in.astype(BF16), wicl_ref[...]))
            t["gate"] = _dot(_sigmoid(gate_in).astype(BF16), wgate_ref[...])
            vsw_ref[r0:r0 + g_rows, :] = _swap_halves(t["v"]).astype(BF16)

        def part_b():
            k, iclr, logw = t["k"], t["iclr"], t["logw"]
            kk = k * kk_ref[...]
            kk = kk * lax.rsqrt(jnp.maximum(headsum(kk * kk), L2_EPS * L2_EPS))
            t["k2"] = k * (1.0 + (iclr - 1.0) * ka_ref[...])
            t["a"] = -kk
            t["b"] = kk * iclr
            logw_hi, logw_lo = _split2(logw)
            tri = tri_ref[0:g_rows, 0:g_rows]
            t["cum"] = _dot(tri, logw_hi) + _dot(tri, logw_lo)

        def part_c():
            cum, logw, r, k2, a, b = t["cum"], t["logw"], t["r"], t["k2"], t["a"], t["b"]
            ends = [cum[(i + 1) * c_len - 1:(i + 1) * c_len, :] for i in range(per_group)]
            cum_end = jnp.concatenate([jnp.broadcast_to(e, (c_len, WIDTH)) for e in ends], axis=0)
            e_neg = jnp.exp(-cum)
            e_tail = jnp.exp(cum_end - cum)
            at = (a * jnp.exp(cum - logw)).astype(BF16)
            rt = (r * jnp.exp(cum)).astype(BF16)
            bt = (b * e_neg).astype(BF16)
            kt = (k2 * e_neg).astype(BF16)
            bh = (b * e_tail).astype(BF16)
            kh = (k2 * e_tail).astype(BF16)
            for i in range(per_group):
                c = g * per_group + i
                src = slice(i * c_len, (i + 1) * c_len)
                top = slice(2 * c * c_len, (2 * c + 1) * c_len)
                bot = slice((2 * c + 1) * c_len, (2 * c + 2) * c_len)
                qa_ref[top, :] = at[src]
                qa_ref[bot, :] = rt[src]
                bk_ref[top, :] = bt[src]
                bk_ref[bot, :] = kt[src]
                bkh_ref[top, :] = bh[src]
                bkh_ref[bot, :] = kh[src]
                pc_ref[c * SUBLANES:c * SUBLANES + 1, :] = jnp.exp(ends[i])
            t["rkr"] = r * k2 * rk_ref[...]

        return [part_a, part_b, part_c]

    ri = lax.broadcasted_iota(jnp.int32, (c_len, 2 * c_len), 0)
    ci = lax.broadcasted_iota(jnp.int32, (c_len, 2 * c_len), 1) % c_len
    strict = ci < ri
    incl = ci <= ri
    lane = lax.broadcasted_iota(jnp.int32, (1, LANES), 1)
    half = [lane < HEAD_DIM, lane >= HEAD_DIM]
    zero_bf = jnp.zeros((), BF16)
    zero_top = jnp.zeros((c_len, LANES), BF16)
    hs = range(HEADS)
    lanes = [slice((h // 2) * LANES, (h // 2 + 1) * LANES) for h in hs]
    own = [half[h % 2] for h in hs]
    other = [half[1 - h % 2] for h in hs]
    steps = int(np.log2(c_len))

    def independent(g, hooks):
        cs = [g * per_group + i for i in range(per_group)]
        items = [(c, h) for c in cs for h in hs]
        pairs = [(c, p) for c in cs for p in range(PAIRS)]
        rows2 = {c: slice(2 * c * c_len, (2 * c + 2) * c_len) for c in cs}
        rows = {c: slice(c * c_len, (c + 1) * c_len) for c in cs}

        def stacked(ref, c, h):
            top = ref[2 * c * c_len:(2 * c + 1) * c_len, lanes[h]]
            bot = ref[(2 * c + 1) * c_len:(2 * c + 2) * c_len, lanes[h]]
            return jnp.concatenate([top, bot] if h % 2 == 0 else [bot, top], axis=0)

        def order(h, first, second):
            return [first, second] if h % 2 == 0 else [second, first]

        qa = {(c, h): jnp.where(own[h], qa_ref[rows2[c], lanes[h]], zero_bf) for c, h in items}
        ab = {(c, h): _dot_nt(qa[c, h], stacked(bk_ref, c, h)) for c, h in items}
        mtop = {k: jnp.where(strict, ab[k][:c_len], 0.0) for k in items}
        mbot = {k: jnp.where(incl, ab[k][c_len:], 0.0).astype(BF16) for k in items}
        vp = {(c, h): jnp.where(other[h], vsw_ref[rows[c], lanes[h]], zero_bf) for c, h in items}
        lkv = {(c, h): _dot(mtop[c, h].astype(BF16),
                            jnp.concatenate(order(h, zero_top, vp[c, h]), axis=0))
               for c, h in items}
        x = {k: qa[k][:c_len].astype(F32) + lkv[k] for k in items}
        xx = {(c, p): jnp.concatenate([x[c, 2 * p], x[c, 2 * p + 1]], axis=1) for c, p in pairs}
        pw = {(c, p): jnp.where(half[0], mtop[c, 2 * p], mtop[c, 2 * p + 1]).astype(BF16)
              for c, p in pairs}
        wide = lax.broadcasted_iota(jnp.int32, (1, 2 * LANES), 1) < LANES
        for j in range(steps):
            xb = {k: xx[k].astype(BF16) for k in pairs}
            xx = {k: xx[k] + _dot(pw[k], jnp.concatenate(
                [jnp.where(wide, xb[k], zero_bf), jnp.where(wide, zero_bf, xb[k])], axis=0))
                for k in pairs}
            if j + 1 < steps:
                pw = {k: _dot(pw[k], jnp.concatenate(
                    [jnp.where(half[0], pw[k], zero_bf), jnp.where(half[1], pw[k], zero_bf)],
                    axis=0)).astype(BF16) for k in pairs}
            for hook in hooks[j]:
                hook()
        x = {(c, h): xx[c, h // 2][:, (h % 2) * LANES:(h % 2 + 1) * LANES] for c, h in items}
        rhs = {(c, h): jnp.concatenate(order(h, x[c, h].astype(BF16), vp[c, h]), axis=0)
               for c, h in items}
        gy = {k: _dot(mbot[k], rhs[k]) for k in items}
        mn = {(c, h): _dot_tn(rhs[c, h], jnp.where(own[h], stacked(bkh_ref, c, h), zero_bf))
              for c, h in items}
        gr = {k: qa[k][c_len:].astype(F32) + gy[k] for k in items}
        return gy, mn, gr

    st = [state_ref[p] for p in range(PAIRS)]

    def dependent(c, res):
        gy, mn, gr = res
        sm = [jnp.where(own[h], st[h // 2], 0.0).astype(BF16) for h in hs]
        upd = [_dot(sm[h], mn[c, h].astype(BF16))
               + mn[c, h][(1 - h % 2) * c_len:(2 - h % 2) * c_len] for h in hs]
        ss = [jnp.concatenate([st[p].astype(BF16)] * 2, axis=0) for p in range(PAIRS)]
        gm = [jnp.where(own[h], gr[c, h], 0.0).astype(BF16) for h in hs]
        yfull = [gy[c, h] + _dot_nt(gm[h], ss[h // 2]) for h in hs]
        for p in range(PAIRS):
            pl_lanes = slice(p * LANES, (p + 1) * LANES)
            pc = pc_ref[c * SUBLANES:c * SUBLANES + 1, pl_lanes]
            st[p] = st[p] * pc + upd[2 * p] + upd[2 * p + 1]
            yacc_ref[c * c_len:(c + 1) * c_len, pl_lanes] = jnp.where(
                half[0], yfull[2 * p + 1], yfull[2 * p])

    tails = [dict() for _ in range(n_groups)]
    for part in prologue_parts(0, tails[0]):
        part()
    prev_res = None
    for g in range(n_groups):
        hooks = {j: [] for j in range(steps)}
        if g + 1 < n_groups:
            for n, part in enumerate(prologue_parts(g + 1, tails[g + 1])):
                hooks[min(steps - 1, 2 * n)].append(part)
        if prev_res is not None:
            for i in range(per_group):
                at_step = min(steps - 1, (i + 1) * steps // (per_group + 1))
                hooks[at_step].append(
                    functools.partial(dependent, (g - 1) * per_group + i, prev_res))
        prev_res = independent(g, hooks)
    for i in range(per_group):
        dependent((n_groups - 1) * per_group + i, prev_res)
    for p in range(PAIRS):
        state_ref[p] = st[p]
    carry_ref[0:1, :] = z_ref[tb - 1:tb, :]

    rkr = jnp.concatenate([t["rkr"] for t in tails], axis=0)
    v = jnp.concatenate([t["v"] for t in tails], axis=0)
    gate = jnp.concatenate([t["gate"] for t in tails], axis=0)
    y = _swap_halves(yacc_ref[...])
    inv_n = 1.0 / HEAD_DIM
    mean = headsum(y) * inv_n
    yc = y - mean
    var = headsum(yc * yc) * inv_n
    yn = yc * lax.rsqrt(var + RWKV_GN_EPS) * gnw_ref[...] + gnb_ref[...]
    bonus = headsum(rkr) * v
    y_ref[...] = ((yn + bonus) * gate).astype(y_ref.dtype)


def _rwkv(zr, batch, seq, mu, dbias, wdec, ibias, wicl, wgate, kk, ka, rk, gnw, gnb):
    tb = min(RWKV_BLOCK, seq)
    n_t = seq // tb
    idx = np.arange(tb)
    same_chunk = (idx[:, None] // RWKV_CHUNK) == (idx[None, :] // RWKV_CHUNK)
    tri = jnp.asarray(same_chunk & (idx[None, :] <= idx[:, None]), BF16)
    hid = np.arange(MXU_TILE) // HEAD_DIM
    bd = jnp.asarray(hid[:, None] == hid[None, :], BF16)
    const = lambda b, i: (0, 0)
    row = lambda b, i: (b * n_t + i, 0)
    vec = lambda n: pl.BlockSpec((1, n), const)
    full = lambda arr: pl.BlockSpec(arr.shape, const)
    return pl.pallas_call(
        _rwkv_kernel,
        grid=(batch, n_t),
        in_specs=[
            pl.BlockSpec((tb, RWKV_IN_PAD), row),
            vec(RWKV_IN_PAD), vec(WIDTH), full(wdec), vec(WIDTH), full(wicl), full(wgate),
            vec(WIDTH), vec(WIDTH), vec(WIDTH), vec(WIDTH), vec(WIDTH),
            full(tri), full(bd),
        ],
        out_specs=pl.BlockSpec((tb, WIDTH), row),
        out_shape=jax.ShapeDtypeStruct((batch * seq, WIDTH), BF16),
        scratch_shapes=[
            pltpu.VMEM((PAIRS, HEAD_DIM, LANES), F32),
            pltpu.VMEM((SUBLANES, RWKV_IN_PAD), F32),
            pltpu.VMEM((2 * tb, WIDTH), BF16),
            pltpu.VMEM((2 * tb, WIDTH), BF16),
            pltpu.VMEM((2 * tb, WIDTH), BF16),
            pltpu.VMEM((tb, WIDTH), BF16),
            pltpu.VMEM((SUBLANES * (tb // RWKV_CHUNK), WIDTH), F32),
            pltpu.VMEM((tb, WIDTH), F32),
        ],
        compiler_params=pltpu.CompilerParams(
            dimension_semantics=("arbitrary", "arbitrary"), vmem_limit_bytes=VMEM_LIMIT),
        name="rwkv7",
    )(zr, mu, dbias, wdec, ibias, wicl, wgate, kk, ka, rk, gnw, gnb, tri, bd)


def _rope(x, cos, sin_signed):
    out = []
    for g in range(x.shape[1] // LANES):
        xg = x[:, g * LANES:(g + 1) * LANES]
        out.append(xg * cos + pltpu.roll(xg, LANES // 2, 1) * sin_signed)
    return out[0] if len(out) == 1 else jnp.concatenate(out, axis=1)


def _fold(x, fn):
    rows, n = x.shape
    parts = x.reshape(rows // SUBLANES, SUBLANES, n)
    level = [parts[i] for i in range(rows // SUBLANES)]
    while len(level) > 1:
        nxt = [fn(level[i], level[i + 1]) for i in range(0, len(level) - 1, 2)]
        if len(level) % 2:
            nxt.append(level[-1])
        level = nxt
    return level[0]


def _pair_layout():
    half = HEAD_DIM // 2
    def pair_cols(base):
        cols = []
        for p in range(PAIRS):
            a, b = base + 2 * p * HEAD_DIM, base + (2 * p + 1) * HEAD_DIM
            cols += [*range(a, a + half), *range(b, b + half),
                     *range(a + half, a + HEAD_DIM), *range(b + half, b + HEAD_DIM)]
        return cols
    def shared_cols(base):
        x1, x2 = list(range(base, base + half)), list(range(base + half, base + HEAD_DIM))
        return x1 + x1 + x2 + x2
    o_k, o_v, o_qi = WIDTH, WIDTH + HEAD_DIM, WIDTH + 2 * HEAD_DIM
    o_ki, o_wi = o_qi + WIDTH, o_qi + WIDTH + HEAD_DIM
    last = [*range(o_v, o_v + HEAD_DIM), *range(o_wi, o_wi + HEADS)]
    last += [-1] * (LANES - len(last))
    order = pair_cols(0) + pair_cols(o_qi) + shared_cols(o_k) + shared_cols(o_ki) + last
    assert len(order) == DSA_IN_PAD
    return np.asarray(order, np.int32)


def _heads_to_rows(x):
    lane = lax.broadcasted_iota(jnp.int32, (1, LANES), 1)
    second = (lane // (HEAD_DIM // 2)) % 2 == 1
    rows = []
    for h in range(HEADS):
        pair = x[:, (h // 2) * LANES:(h // 2 + 1) * LANES]
        rows.append(jnp.where(second == (h % 2 == 1), pair, 0.0))
    return jnp.concatenate(rows, axis=0)


def _dsa_kernel(qq_ref, k_ref, ki_ref, vw_ref, wq_ref, cosq_ref, sinq_ref, cosk_ref, sink_ref,
                o_ref,
                k_s, ki_s, vt_s, sc_s, s_all, acc_s, *, topk, idx_scale):
    j = pl.program_id(1)
    tq = qq_ref.shape[0]
    seq = k_ref.shape[0]
    kc = min(KEY_CHUNK, seq)
    n_ch = (j * tq + tq + kc - 1) // kc
    inf = jnp.float32(jnp.inf)

    @pl.when(j == 0)
    def _():
        k_s[...] = _rope(k_ref[...], cosk_ref[...], sink_ref[...]).astype(BF16)
        ki_s[...] = _rope(ki_ref[...], cosk_ref[...], sink_ref[...]).astype(BF16)
        vwt = vw_ref[...].T
        for c in range(seq // kc):
            vt_s[c] = vwt[:HEAD_DIM, c * kc:(c + 1) * kc].astype(BF16)

    cosq = cosq_ref[...]
    sinq = sinq_ref[...]
    q_all = _heads_to_rows(_rope(qq_ref[:, 0:WIDTH], cosq, sinq)
                           * (HEAD_DIM ** -0.5)).astype(BF16)
    qi_all = _heads_to_rows(_rope(qq_ref[:, WIDTH:2 * WIDTH], cosq, sinq)).astype(BF16)

    sel_r = lax.broadcasted_iota(jnp.int32, (2 * SUBLANES, LANES), 0)
    sel_c = lax.broadcasted_iota(jnp.int32, (2 * SUBLANES, LANES), 1)
    sel = (sel_c == sel_r + HEAD_DIM).astype(BF16)
    w1, w2, w3 = _split3(wq_ref[...])
    w_t = (_dot_nt(sel, w1) + (_dot_nt(sel, w2) + _dot_nt(sel, w3))) * idx_scale

    qpos = j * tq + lax.broadcasted_iota(jnp.int32, (1, tq), 1)
    key_in_chunk = lax.broadcasted_iota(jnp.int32, (kc, tq), 0)

    def chunk_rows(c):
        return pl.ds(pl.multiple_of(c * kc, kc), kc)

    def paired_loop(step, carry):
        w = DSA_CHUNKS_PER_TRIP
        carry = lax.fori_loop(
            0, n_ch // w, lambda t, cr: step([w * t + i for i in range(w)], cr), carry)
        width = w // 2
        while width >= 1:
            base = (n_ch // (2 * width)) * (2 * width)
            carry = lax.fori_loop(
                0, (n_ch // width) % 2,
                lambda _, cr, base=base, width=width: step([base + i for i in range(width)], cr),
                carry)
            width //= 2
        return carry

    def score_step(cs, carry):
        rmax, rmin, n_ge0, n_gt0 = carry
        lg = [[_dot_nt(ki_s[chunk_rows(c), :], qi_all[h * tq:(h + 1) * tq])
               for h in range(HEADS)] for c in cs]
        for i, c in enumerate(cs):
            acc = w_t[0:1] * jnp.maximum(lg[i][0], 0.0)
            for h in range(1, HEADS):
                acc = acc + w_t[h:h + 1] * jnp.maximum(lg[i][h], 0.0)
            causal = (c * kc + key_in_chunk) <= qpos
            lo_fill = jnp.where(causal, acc, -inf)
            sc_s[chunk_rows(c), :] = lo_fill
            rmax = jnp.maximum(rmax, _fold(lo_fill, jnp.maximum))
            rmin = jnp.minimum(rmin, _fold(jnp.where(causal, acc, inf), jnp.minimum))
            n_ge0 = n_ge0 + _fold((lo_fill >= 0.0).astype(F32), jnp.add)
            n_gt0 = n_gt0 + _fold((lo_fill > 0.0).astype(F32), jnp.add)
        return rmax, rmin, n_ge0, n_gt0

    zeros8 = jnp.zeros((SUBLANES, tq), F32)
    rmax, rmin, n_ge0, n_gt0 = paired_loop(
        score_step,
        (jnp.full((SUBLANES, tq), -inf, F32), jnp.full((SUBLANES, tq), inf, F32), zeros8, zeros8))
    rmax = jnp.max(rmax, axis=0, keepdims=True)
    rmin = jnp.min(rmin, axis=0, keepdims=True)
    c_ge0 = jnp.sum(n_ge0, axis=0, keepdims=True)
    c_gt0 = jnp.sum(n_gt0, axis=0, keepdims=True)

    def count(cmp, x):
        xb = jnp.broadcast_to(x, (SUBLANES, tq))

        def body(c, accs):
            parts = sc_s[chunk_rows(c), :].reshape(kc // SUBLANES, SUBLANES, tq)
            accs = list(accs)
            for i in range(kc // SUBLANES):
                a = accs[i % COUNT_CHAINS]
                accs[i % COUNT_CHAINS] = jnp.where(cmp(parts[i], xb), a + 1.0, a)
            return tuple(accs)

        accs = lax.fori_loop(0, n_ch, body,
                             tuple(jnp.zeros((SUBLANES, tq), F32) for _ in range(COUNT_CHAINS)))
        return jnp.sum(_fold(jnp.concatenate(accs, axis=0), jnp.add), axis=0, keepdims=True)

    kf = jnp.float32(topk)
    n_causal = (qpos + 1).astype(F32)
    all_keys = n_causal < kf
    ge = lambda s, x: s >= x
    gt = lambda s, x: s > x
    zero = jnp.zeros((1, tq), F32)
    c_max = count(ge, rmax)
    top_tied = c_max >= kf
    zero_thr = (c_gt0 < kf) & (c_ge0 >= kf)
    above0 = c_gt0 >= kf
    lo = jnp.where(top_tied, rmax, jnp.where(zero_thr | above0, zero, rmin))
    c_lo = jnp.where(top_tied, c_max, jnp.where(zero_thr | above0, c_ge0, n_causal))
    hi = jnp.where(top_tied | above0, rmax, zero)
    done = all_keys | top_tied | zero_thr | (c_lo == kf) | (lo == hi)

    def search_cond(st):
        _, _, _, done_f, it = st
        return jnp.logical_and(jnp.min(done_f) < 0.5, it < SEARCH_ITERS_MAX)

    def search_body(st):
        lo, hi, c_lo, done_f, it = st
        for _ in range(SEARCH_UNROLL):
            active = done_f < 0.5
            mid = lo * 0.5 + hi * 0.5
            c = count(ge, mid)
            enough = c >= kf
            stuck = (mid == lo) | (mid == hi)
            lo = jnp.where(active & enough, mid, lo)
            hi = jnp.where(active & jnp.logical_not(enough), mid, hi)
            c_lo = jnp.where(active & enough, c, c_lo)
            done_f = jnp.where(active & ((c_lo == kf) | stuck), jnp.float32(1.0), done_f)
        return lo, hi, c_lo, done_f, it + SEARCH_UNROLL

    lo, hi, c_lo, _, _ = lax.while_loop(
        search_cond, search_body, (lo, hi, c_lo, done.astype(F32), jnp.int32(0)))
    lo = jnp.where(all_keys, -inf, lo)
    exact_k = all_keys | (c_lo == kf)
    no_ties = jnp.min(exact_k.astype(F32)) > 0.5

    @pl.when(no_ties)
    def _():
        def body(c, carry):
            s = sc_s[chunk_rows(c), :]
            causal = (c * kc + key_in_chunk) <= qpos
            sc_s[chunk_rows(c), :] = jnp.where((s >= lo) & causal, 0.0, MASK_BIAS)
            return carry
        lax.fori_loop(0, n_ch, body, 0)

    @pl.when(jnp.logical_not(no_ties))
    def _():
        def snap_body(c, acc):
            s = sc_s[chunk_rows(c), :]
            return jnp.minimum(acc, _fold(jnp.where(s >= lo, s, inf), jnp.minimum))
        thr = jnp.min(lax.fori_loop(0, n_ch, snap_body, jnp.full((SUBLANES, tq), inf, F32)),
                      axis=0, keepdims=True)
        need = kf - count(gt, thr)
        ki_ = lax.broadcasted_iota(jnp.int32, (kc, kc), 0)
        kj_ = lax.broadcasted_iota(jnp.int32, (kc, kc), 1)
        earlier = (kj_ < ki_).astype(BF16)
        ones = jnp.ones((2 * SUBLANES, kc), BF16)

        def body(c, run):
            s = sc_s[chunk_rows(c), :]
            causal = (c * kc + key_in_chunk) <= qpos
            eq = (s == thr).astype(BF16)
            rank = run + _dot(earlier, eq)
            take = (s > thr) | ((s == thr) & (rank < need))
            sc_s[chunk_rows(c), :] = jnp.where(take & causal, 0.0, MASK_BIAS)
            return run + _dot(ones, eq)[0:1]
        lax.fori_loop(0, n_ch, body, jnp.zeros((1, tq), F32))

    def logits_step(cs, m_acc):
        s = [[_dot_nt(k_s[chunk_rows(c), :], q_all[h * tq:(h + 1) * tq])
              for h in range(HEADS)] for c in cs]
        for i, c in enumerate(cs):
            bias = sc_s[chunk_rows(c), :]
            out = []
            for h in range(HEADS):
                sh = s[i][h] + bias
                s_all[chunk_rows(c), h * tq:(h + 1) * tq] = sh
                out.append(_fold(sh, jnp.maximum))
            m_acc = jnp.maximum(m_acc, jnp.concatenate(out, axis=1))
        return m_acc

    m = jnp.max(paired_loop(logits_step, jnp.full((SUBLANES, HEADS * tq), MASK_BIAS, F32)),
                axis=0, keepdims=True)

    def pv_step(cs, l_acc):
        out = []
        for h in range(HEADS):
            cols = slice(h * tq, (h + 1) * tq)
            p = [jnp.exp(s_all[chunk_rows(c), cols] - m[:, cols]) for c in cs]
            fold = _fold(p[0], jnp.add)
            upd = _dot(vt_s[cs[0]], p[0].astype(BF16))
            for i in range(1, len(cs)):
                fold = fold + _fold(p[i], jnp.add)
                upd = upd + _dot(vt_s[cs[i]], p[i].astype(BF16))
            out.append(fold)
            acc_s[:, cols] += upd
        return l_acc + jnp.concatenate(out, axis=1)

    acc_s[...] = jnp.zeros_like(acc_s)
    l_acc = paired_loop(pv_step, jnp.zeros((SUBLANES, HEADS * tq), F32))
    out_t = acc_s[...] / jnp.sum(l_acc, axis=0, keepdims=True)
    for p in range(PAIRS):
        pair = jnp.concatenate([out_t[:, 2 * p * tq:(2 * p + 1) * tq],
                                out_t[:, (2 * p + 1) * tq:(2 * p + 2) * tq]], axis=0)
        o_ref[:, p * LANES:(p + 1) * LANES] = pair.T.astype(o_ref.dtype)


def _dsa(zd, batch, seq, cos_t, sin_t):
    tq = min(Q_TILE, seq)
    nb = seq // tq
    topk = min(TOPK_MAX, seq // 4)
    kc = min(KEY_CHUNK, seq)
    idx_scale = HEADS ** -0.5 * HEAD_DIM ** -0.5
    kernel = functools.partial(_dsa_kernel, topk=topk, idx_scale=idx_scale)
    qcols = 2 * WIDTH // LANES
    return pl.pallas_call(
        kernel,
        grid=(batch, nb),
        in_specs=[
            pl.BlockSpec((tq, 2 * WIDTH), lambda b, j: (b * nb + j, 0)),
            pl.BlockSpec((seq, LANES), lambda b, j: (b, qcols)),
            pl.BlockSpec((seq, LANES), lambda b, j: (b, qcols + 1)),
            pl.BlockSpec((seq, LANES), lambda b, j: (b, qcols + 2)),
            pl.BlockSpec((tq, LANES), lambda b, j: (b * nb + j, qcols + 2)),
            pl.BlockSpec((tq, LANES), lambda b, j: (j, 0)),
            pl.BlockSpec((tq, LANES), lambda b, j: (j, 0)),
            pl.BlockSpec((seq, LANES), lambda b, j: (0, 0)),
            pl.BlockSpec((seq, LANES), lambda b, j: (0, 0)),
        ],
        out_specs=pl.BlockSpec((tq, WIDTH), lambda b, j: (b * nb + j, 0)),
        out_shape=jax.ShapeDtypeStruct((batch * seq, WIDTH), BF16),
        scratch_shapes=[
            pltpu.VMEM((seq, LANES), BF16),
            pltpu.VMEM((seq, LANES), BF16),
            pltpu.VMEM((seq // kc, HEAD_DIM, kc), BF16),
            pltpu.VMEM((seq, tq), F32),
            pltpu.VMEM((seq, HEADS * tq), F32),
            pltpu.VMEM((HEAD_DIM, HEADS * tq), F32),
        ],
        compiler_params=pltpu.CompilerParams(
            dimension_semantics=("arbitrary", "arbitrary"), vmem_limit_bytes=VMEM_LIMIT),
        name="dsa",
    )(zd, zd, zd, zd, zd, cos_t, sin_t, cos_t, sin_t)


def _merge_ffn_kernel(x_ref, ya_ref, yb_ref, zg_ref, wba_ref, wbb_ref, wout_ref, gffn_ref,
                      wup_ref, wdown_ref, gfin_ref, o_ref, *, final, hidden_chunk):
    d = x_ref.shape[1]
    pa = _dot(ya_ref[...], wba_ref[...])
    pb = _dot(yb_ref[...], wbb_ref[...])
    merged = _sigmoid(zg_ref[:, 0:d]) * pa + _sigmoid(zg_ref[:, d:2 * d]) * pb
    x1 = x_ref[...] + _dot(merged.astype(BF16), wout_ref[...])
    h2 = _rms(x1, gffn_ref[...]).astype(BF16)
    x2 = x1
    hidden = wup_ref.shape[1]
    for c in range(hidden // hidden_chunk):
        sl = slice(c * hidden_chunk, (c + 1) * hidden_chunk)
        u = jnp.maximum(_dot(h2, wup_ref[:, sl]), 0.0)
        x2 = x2 + _dot((u * u).astype(BF16), wdown_ref[sl, :])
    o_ref[...] = _rms(x2, gfin_ref[...]) if final else x2


def _merge_ffn(x2, ya, yb, zg, wba, wbb, wout, gffn, wup, wdown, gfin, tm, final):
    n, d = x2.shape
    const = lambda i: (0, 0)
    row = lambda i: (i, 0)
    resident = lambda arr: pl.BlockSpec(arr.shape, const, pipeline_mode=pl.Buffered(1))
    kernel = functools.partial(_merge_ffn_kernel, final=final,
                               hidden_chunk=min(1024, wup.shape[1]))
    return pl.pallas_call(
        kernel,
        grid=(n // tm,),
        in_specs=[
            pl.BlockSpec((tm, d), row),
            pl.BlockSpec((tm, WIDTH), row),
            pl.BlockSpec((tm, WIDTH), row),
            pl.BlockSpec((tm, 2 * d), row),
            resident(wba), resident(wbb), resident(wout),
            pl.BlockSpec((1, d), const),
            resident(wup), resident(wdown),
            pl.BlockSpec((1, d), const),
        ],
        out_specs=pl.BlockSpec((tm, d), row),
        out_shape=jax.ShapeDtypeStruct((n, d), F32),
        compiler_params=pltpu.CompilerParams(
            dimension_semantics=("arbitrary",), vmem_limit_bytes=VMEM_LIMIT),
        name="merge_ffn",
    )(x2, ya, yb, zg, wba, wbb, wout, gffn, wup, wdown, gfin)


def _rope_tables(seq):
    inv = 1.0 / (ROPE_THETA ** (jnp.arange(0, HEAD_DIM, 2, dtype=F32) / HEAD_DIM))
    ang = jnp.arange(seq, dtype=F32)[:, None] * inv[None, :]
    cos, sin = jnp.cos(ang), jnp.sin(ang)
    cos_t = jnp.concatenate([cos, cos, cos, cos], axis=1)
    sin_t = jnp.concatenate([-sin, -sin, sin, sin], axis=1)
    return cos_t, sin_t


def _pad_cols(a, n):
    return jnp.pad(a, ((0, 0), (0, n - a.shape[1])))


def _pad_rows(a, n):
    return jnp.pad(a, ((0, n - a.shape[0]), (0, 0)))


def kernel(x, g_mix, w_in, mu_shift, decay_bias, w_decay_up, iclr_bias, w_iclr_up, w_gate_up,
           k_k, k_a, r_k, gn_w, gn_b, w_branch, w_out, g_ffn, w_ffn_up, w_ffn_down, g_final):
    batch, seq, d = x.shape
    depth = g_mix.shape[0]
    n = batch * seq
    tm = 512
    assert seq % min(Q_TILE, seq) == 0 and seq % min(KEY_CHUNK, seq) == 0
    assert seq % min(RWKV_BLOCK, seq) == 0
    assert min(RWKV_BLOCK, seq) % (RWKV_GROUPS * RWKV_CHUNK) == 0
    assert n % tm == 0
    cos_t, sin_t = _rope_tables(seq)
    row = lambda a: a.reshape(1, -1)
    xf = x.reshape(n, d)
    for l in range(depth):
        w = w_in[l]
        w_rwkv = _pad_cols(w[:, :RWKV_IN], RWKV_IN_PAD).astype(BF16)
        wd = _pad_cols(w[:, RWKV_IN:RWKV_IN + DSA_IN], DSA_IN + 1)
        order = _pair_layout()
        w_dsa = jnp.take(wd, jnp.asarray(np.where(order < 0, DSA_IN, order)),
                         axis=1).astype(BF16)
        w_gate = w[:, RWKV_IN + DSA_IN:].astype(BF16)
        zr, zd, zg = _inproj(xf, row(g_mix[l]), w_rwkv, w_dsa, w_gate, tm)

        wdec = _pad_rows(w_decay_up[l], LANES).astype(BF16)
        wicl = jnp.pad(w_iclr_up[l], ((DECAY_LORA, 0), (0, 0))).astype(BF16)
        wgate = _pad_rows(w_gate_up[l], GATE_LORA_PAD).astype(BF16)
        mu = _pad_cols(row(mu_shift[l]), RWKV_IN_PAD)
        ya = _rwkv(zr, batch, seq, mu, row(decay_bias[l]), wdec, row(iclr_bias[l]), wicl,
                   wgate, row(k_k[l]), row(k_a[l]), row(r_k[l]), row(gn_w[l]), row(gn_b[l]))
        yb = _dsa(zd, batch, seq, cos_t, sin_t)

        xf = _merge_ffn(xf, ya, yb, zg, w_branch[l, 0].astype(BF16),
                        w_branch[l, 1].astype(BF16), w_out[l].astype(BF16), row(g_ffn[l]),
                        w_ffn_up[l].astype(BF16), w_ffn_down[l].astype(BF16), row(g_final),
                        tm, final=(l == depth - 1))
    return xf.reshape(batch, seq, d)
```

```python
import functools

import numpy as np
import jax
import jax.numpy as jnp
from jax import lax
from jax.experimental import pallas as pl
from jax.experimental.pallas import tpu as pltpu

F32 = jnp.float32
BF16 = jnp.bfloat16

HEADS = 8
HEAD_DIM = 64
PAIRS = HEADS // 2
WIDTH = HEADS * HEAD_DIM
DECAY_LORA = 64
ICLR_LORA = 64
GATE_LORA = 160
GATE_LORA_PAD = 256
RWKV_IN = 3 * WIDTH + DECAY_LORA + ICLR_LORA + GATE_LORA
RWKV_IN_PAD = 3 * WIDTH + DECAY_LORA + ICLR_LORA + GATE_LORA_PAD
DSA_IN = WIDTH + 2 * HEAD_DIM + WIDTH + HEAD_DIM + HEADS
DSA_IN_PAD = 2 * WIDTH + 3 * 128
TOPK_MAX = 256
Q_TILE = 256
V_ROWS = HEAD_DIM + 16
ROPE_THETA = 10000.0
NORM_EPS = 1e-6
RWKV_GN_EPS = 64e-5
L2_EPS = 1e-12

LANES = 128
SUBLANES = 8
MXU_TILE = 256
VMEM_LIMIT = 56 * 1024 * 1024

RWKV_CHUNK = 64
RWKV_BLOCK = 256
KEY_CHUNK = 256
DSA_CHUNKS_PER_TRIP = 4
SEARCH_UNROLL = 4
SEARCH_ITERS_MAX = 320
COUNT_CHAINS = 4
MASK_BIAS = -1e30


def _dot(a, b):
    return jnp.dot(a, b, preferred_element_type=F32)


def _dot_nt(a, b):
    return lax.dot_general(a, b, (((1,), (1,)), ((), ())), preferred_element_type=F32)


def _dot_tn(a, b):
    return lax.dot_general(a, b, (((0,), (0,)), ((), ())), preferred_element_type=F32)


def _split2(x):
    hi = x.astype(BF16)
    lo = (x - hi.astype(F32)).astype(BF16)
    return hi, lo


def _split3(x):
    h1 = x.astype(BF16)
    r1 = x - h1.astype(F32)
    h2 = r1.astype(BF16)
    h3 = (r1 - h2.astype(F32)).astype(BF16)
    return h1, h2, h3


def _rms(x, g):
    ms = jnp.mean(x * x, axis=-1, keepdims=True)
    return x * lax.rsqrt(ms + NORM_EPS) * g


def _sigmoid(x):
    return 1.0 / (1.0 + jnp.exp(-x))


def _swap_halves(x):
    cols = [pltpu.roll(x[:, g * LANES:(g + 1) * LANES], HEAD_DIM, 1)
            for g in range(x.shape[1] // LANES)]
    return jnp.concatenate(cols, axis=1)


def _inproj_kernel(x_ref, g_ref, wr_ref, wd_ref, wg_ref, zr_ref, zd_ref, zg_ref):
    h = _rms(x_ref[...], g_ref[...]).astype(BF16)
    zr_ref[...] = _dot(h, wr_ref[...])
    zd_ref[...] = _dot(h, wd_ref[...])
    zg_ref[...] = _dot(h, wg_ref[...])


def _inproj(x2, g, wr, wd, wg, tm):
    n, d = x2.shape
    const = lambda i: (0, 0)
    row = lambda i: (i, 0)
    return pl.pallas_call(
        _inproj_kernel,
        grid=(n // tm,),
        in_specs=[
            pl.BlockSpec((tm, d), row),
            pl.BlockSpec((1, d), const),
            pl.BlockSpec(wr.shape, const),
            pl.BlockSpec(wd.shape, const),
            pl.BlockSpec(wg.shape, const),
        ],
        out_specs=[
            pl.BlockSpec((tm, wr.shape[1]), row),
            pl.BlockSpec((tm, wd.shape[1]), row),
            pl.BlockSpec((tm, wg.shape[1]), row),
        ],
        out_shape=[
            jax.ShapeDtypeStruct((n, wr.shape[1]), F32),
            jax.ShapeDtypeStruct((n, wd.shape[1]), F32),
            jax.ShapeDtypeStruct((n, wg.shape[1]), F32),
        ],
        compiler_params=pltpu.CompilerParams(
            dimension_semantics=("arbitrary",), vmem_limit_bytes=VMEM_LIMIT),
        name="inproj",
    )(x2, g, wr, wd, wg)


def _rwkv_kernel(z_ref, mu_ref, dbias_ref, wdec_ref, ibias_ref, wicl_ref, wgate_ref,
                 kk_ref, ka_ref, rk_ref, gnw_ref, gnb_ref, tri_ref, bd_ref,
                 y_ref,
                 state_ref, carry_ref, qa_ref, bk_ref, bkh_ref, vsw_ref, pc_ref, yacc_ref):
    tb = z_ref.shape[0]
    c_len = RWKV_CHUNK
    n_chunks = tb // c_len

    @pl.when(pl.program_id(1) == 0)
    def _():
        state_ref[...] = jnp.zeros_like(state_ref)
        carry_ref[...] = jnp.zeros_like(carry_ref)

    per_group = n_chunks
    g_rows = tb
    bd = bd_ref[...]
    bd_w = bd.shape[0]

    def headsum(t):
        tb16 = t.astype(BF16)
        return jnp.concatenate(
            [_dot(tb16[:, i:i + bd_w], bd) for i in range(0, WIDTH, bd_w)], axis=1)

    def prologue():
        g, r0, t = 0, 0, {}

        def part_a():
            z = z_ref[r0:r0 + g_rows, :]
            before = carry_ref[0:1, :]
            first_row = lax.broadcasted_iota(jnp.int32, (g_rows, 1), 0) == 0
            prev = jnp.where(first_row, before, pltpu.roll(z, 1, 0))
            zs = z + (prev - z) * mu_ref[...]
            t["r"] = zs[:, 0:WIDTH]
            t["k"] = zs[:, WIDTH:2 * WIDTH]
            t["v"] = zs[:, 2 * WIDTH:3 * WIDTH]
            lora_in = zs[:, 3 * WIDTH:3 * WIDTH + LANES]
            gate_in = zs[:, 3 * WIDTH + LANES:]
            dec_pre = dbias_ref[...] + _dot(jnp.tanh(lora_in).astype(BF16), wdec_ref[...])
            neg = -dec_pre
            softplus = jnp.maximum(neg, 0.0) + jnp.log(1.0 + jnp.exp(-jnp.abs(neg)))
            w_log = -softplus - 0.5
            t["logw"] = -jnp.exp(w_log)
            t["iclr"] = _sigmoid(ibias_ref[...] + _dot(lora_in.astype(BF16), wicl_ref[...]))
            t["gate"] = _dot(_sigmoid(gate_in).astype(BF16), wgate_ref[...])
            vsw_ref[r0:r0 + g_rows, :] = _swap_halves(t["v"]).astype(BF16)

        def part_b():
            k, iclr, logw = t["k"], t["iclr"], t["logw"]
            kk = k * kk_ref[...]
            kk = kk * lax.rsqrt(jnp.maximum(headsum(kk * kk), L2_EPS * L2_EPS))
            t["k2"] = k * (1.0 + (iclr - 1.0) * ka_ref[...])
            t["a"] = -kk
            t["b"] = kk * iclr
            logw_hi, logw_lo = _split2(logw)
            tri = tri_ref[0:g_rows, 0:g_rows]
            t["cum"] = _dot(tri, logw_hi) + _dot(tri, logw_lo)

        def part_c():
            cum, logw, r, k2, a, b = t["cum"], t["logw"], t["r"], t["k2"], t["a"], t["b"]
            ends = [cum[(i + 1) * c_len - 1:(i + 1) * c_len, :] for i in range(per_group)]
            cum_end = jnp.concatenate([jnp.broadcast_to(e, (c_len, WIDTH)) for e in ends], axis=0)
            e_neg = jnp.exp(-cum)
            e_tail = jnp.exp(cum_end - cum)
            at = (a * jnp.exp(cum - logw)).astype(BF16)
            rt = (r * jnp.exp(cum)).astype(BF16)
            bt = (b * e_neg).astype(BF16)
            kt = (k2 * e_neg).astype(BF16)
            bh = (b * e_tail).astype(BF16)
            kh = (k2 * e_tail).astype(BF16)
            for i in range(per_group):
                c = g * per_group + i
                src = slice(i * c_len, (i + 1) * c_len)
                top = slice(2 * c * c_len, (2 * c + 1) * c_len)
                bot = slice((2 * c + 1) * c_len, (2 * c + 2) * c_len)
                qa_ref[top, :] = at[src]
                qa_ref[bot, :] = rt[src]
                bk_ref[top, :] = bt[src]
                bk_ref[bot, :] = kt[src]
                bkh_ref[top, :] = bh[src]
                bkh_ref[bot, :] = kh[src]
                pc_ref[c * SUBLANES:c * SUBLANES + 1, :] = jnp.exp(ends[i])
            t["rkr"] = r * k2 * rk_ref[...]

        part_a()
        part_b()
        part_c()
        return t

    ri = lax.broadcasted_iota(jnp.int32, (c_len, 2 * c_len), 0)
    ci = lax.broadcasted_iota(jnp.int32, (c_len, 2 * c_len), 1) % c_len
    strict = ci < ri
    incl = ci <= ri
    lane = lax.broadcasted_iota(jnp.int32, (1, LANES), 1)
    half = [lane < HEAD_DIM, lane >= HEAD_DIM]
    zero_bf = jnp.zeros((), BF16)
    zero_top = jnp.zeros((c_len, LANES), BF16)
    hs = range(HEADS)
    lanes = [slice((h // 2) * LANES, (h // 2 + 1) * LANES) for h in hs]
    own = [half[h % 2] for h in hs]
    other = [half[1 - h % 2] for h in hs]
    steps = int(np.log2(c_len))

    def independent():
        cs = list(range(n_chunks))
        items = [(c, h) for c in cs for h in hs]
        pairs = [(c, p) for c in cs for p in range(PAIRS)]
        rows2 = {c: slice(2 * c * c_len, (2 * c + 2) * c_len) for c in cs}
        rows = {c: slice(c * c_len, (c + 1) * c_len) for c in cs}

        def stacked(ref, c, h):
            top = ref[2 * c * c_len:(2 * c + 1) * c_len, lanes[h]]
            bot = ref[(2 * c + 1) * c_len:(2 * c + 2) * c_len, lanes[h]]
            return jnp.concatenate([top, bot] if h % 2 == 0 else [bot, top], axis=0)

        def order(h, first, second):
            return [first, second] if h % 2 == 0 else [second, first]

        qa = {(c, h): jnp.where(own[h], qa_ref[rows2[c], lanes[h]], zero_bf) for c, h in items}
        ab = {(c, h): _dot_nt(qa[c, h], stacked(bk_ref, c, h)) for c, h in items}
        mtop = {k: jnp.where(strict, ab[k][:c_len], 0.0) for k in items}
        mbot = {k: jnp.where(incl, ab[k][c_len:], 0.0).astype(BF16) for k in items}
        vp = {(c, h): jnp.where(other[h], vsw_ref[rows[c], lanes[h]], zero_bf) for c, h in items}
        lkv = {(c, h): _dot(mtop[c, h].astype(BF16),
                            jnp.concatenate(order(h, zero_top, vp[c, h]), axis=0))
               for c, h in items}
        x = {k: qa[k][:c_len].astype(F32) + lkv[k] for k in items}
        xx = {(c, p): jnp.concatenate([x[c, 2 * p], x[c, 2 * p + 1]], axis=1) for c, p in pairs}
        pw = {(c, p): jnp.where(half[0], mtop[c, 2 * p], mtop[c, 2 * p + 1]).astype(BF16)
              for c, p in pairs}
        wide = lax.broadcasted_iota(jnp.int32, (1, 2 * LANES), 1) < LANES
        for j in range(steps):
            xb = {k: xx[k].astype(BF16) for k in pairs}
            xx = {k: xx[k] + _dot(pw[k], jnp.concatenate(
                [jnp.where(wide, xb[k], zero_bf), jnp.where(wide, zero_bf, xb[k])], axis=0))
                for k in pairs}
            if j + 1 < steps:
                pw = {k: _dot(pw[k], jnp.concatenate(
                    [jnp.where(half[0], pw[k], zero_bf), jnp.where(half[1], pw[k], zero_bf)],
                    axis=0)).astype(BF16) for k in pairs}
        x = {(c, h): xx[c, h // 2][:, (h % 2) * LANES:(h % 2 + 1) * LANES] for c, h in items}
        rhs = {(c, h): jnp.concatenate(order(h, x[c, h].astype(BF16), vp[c, h]), axis=0)
               for c, h in items}
        gy = {k: _dot(mbot[k], rhs[k]) for k in items}
        mn = {(c, h): _dot_tn(rhs[c, h], jnp.where(own[h], stacked(bkh_ref, c, h), zero_bf))
              for c, h in items}
        gr = {k: qa[k][c_len:].astype(F32) + gy[k] for k in items}
        return gy, mn, gr

    st = [state_ref[p] for p in range(PAIRS)]

    def dependent(c, res):
        gy, mn, gr = res
        sm = [jnp.where(own[h], st[h // 2], 0.0).astype(BF16) for h in hs]
        upd = [_dot(sm[h], mn[c, h].astype(BF16))
               + mn[c, h][(1 - h % 2) * c_len:(2 - h % 2) * c_len] for h in hs]
        ss = [jnp.concatenate([st[p].astype(BF16)] * 2, axis=0) for p in range(PAIRS)]
        gm = [jnp.where(own[h], gr[c, h], 0.0).astype(BF16) for h in hs]
        yfull = [gy[c, h] + _dot_nt(gm[h], ss[h // 2]) for h in hs]
        for p in range(PAIRS):
            pl_lanes = slice(p * LANES, (p + 1) * LANES)
            pc = pc_ref[c * SUBLANES:c * SUBLANES + 1, pl_lanes]
            st[p] = st[p] * pc + upd[2 * p] + upd[2 * p + 1]
            yacc_ref[c * c_len:(c + 1) * c_len, pl_lanes] = jnp.where(
                half[0], yfull[2 * p + 1], yfull[2 * p])

    tail = prologue()
    res = independent()
    for c in range(n_chunks):
        dependent(c, res)
    for p in range(PAIRS):
        state_ref[p] = st[p]
    carry_ref[0:1, :] = z_ref[tb - 1:tb, :]

    rkr, v, gate = tail["rkr"], tail["v"], tail["gate"]
    y = _swap_halves(yacc_ref[...])
    inv_n = 1.0 / HEAD_DIM
    mean = headsum(y) * inv_n
    yc = y - mean
    var = headsum(yc * yc) * inv_n
    yn = yc * lax.rsqrt(var + RWKV_GN_EPS) * gnw_ref[...] + gnb_ref[...]
    bonus = headsum(rkr) * v
    y_ref[...] = ((yn + bonus) * gate).astype(y_ref.dtype)


def _rwkv(zr, batch, seq, mu, dbias, wdec, ibias, wicl, wgate, kk, ka, rk, gnw, gnb):
    tb = min(RWKV_BLOCK, seq)
    n_t = seq // tb
    idx = np.arange(tb)
    same_chunk = (idx[:, None] // RWKV_CHUNK) == (idx[None, :] // RWKV_CHUNK)
    tri = jnp.asarray(same_chunk & (idx[None, :] <= idx[:, None]), BF16)
    hid = np.arange(MXU_TILE) // HEAD_DIM
    bd = jnp.asarray(hid[:, None] == hid[None, :], BF16)
    const = lambda b, i: (0, 0)
    row = lambda b, i: (b * n_t + i, 0)
    vec = lambda n: pl.BlockSpec((1, n), const)
    full = lambda arr: pl.BlockSpec(arr.shape, const)
    return pl.pallas_call(
        _rwkv_kernel,
        grid=(batch, n_t),
        in_specs=[
            pl.BlockSpec((tb, RWKV_IN_PAD), row),
            vec(RWKV_IN_PAD), vec(WIDTH), full(wdec), vec(WIDTH), full(wicl), full(wgate),
            vec(WIDTH), vec(WIDTH), vec(WIDTH), vec(WIDTH), vec(WIDTH),
            full(tri), full(bd),
        ],
        out_specs=pl.BlockSpec((tb, WIDTH), row),
        out_shape=jax.ShapeDtypeStruct((batch * seq, WIDTH), BF16),
        scratch_shapes=[
            pltpu.VMEM((PAIRS, HEAD_DIM, LANES), F32),
            pltpu.VMEM((SUBLANES, RWKV_IN_PAD), F32),
            pltpu.VMEM((2 * tb, WIDTH), BF16),
            pltpu.VMEM((2 * tb, WIDTH), BF16),
            pltpu.VMEM((2 * tb, WIDTH), BF16),
            pltpu.VMEM((tb, WIDTH), BF16),
            pltpu.VMEM((SUBLANES * (tb // RWKV_CHUNK), WIDTH), F32),
            pltpu.VMEM((tb, WIDTH), F32),
        ],
        compiler_params=pltpu.CompilerParams(
            dimension_semantics=("arbitrary", "arbitrary"), vmem_limit_bytes=VMEM_LIMIT),
        name="rwkv7",
    )(zr, mu, dbias, wdec, ibias, wicl, wgate, kk, ka, rk, gnw, gnb, tri, bd)


def _rope(x, cos, sin_signed):
    out = []
    for g in range(x.shape[1] // LANES):
        xg = x[:, g * LANES:(g + 1) * LANES]
        out.append(xg * cos + pltpu.roll(xg, LANES // 2, 1) * sin_signed)
    return out[0] if len(out) == 1 else jnp.concatenate(out, axis=1)


def _fold(x, fn):
    rows, n = x.shape
    parts = x.reshape(rows // SUBLANES, SUBLANES, n)
    level = [parts[i] for i in range(rows // SUBLANES)]
    while len(level) > 1:
        nxt = [fn(level[i], level[i + 1]) for i in range(0, len(level) - 1, 2)]
        if len(level) % 2:
            nxt.append(level[-1])
        level = nxt
    return level[0]


def _pair_layout():
    half = HEAD_DIM // 2
    def pair_cols(base):
        cols = []
        for p in range(PAIRS):
            a, b = base + 2 * p * HEAD_DIM, base + (2 * p + 1) * HEAD_DIM
            cols += [*range(a, a + half), *range(b, b + half),
                     *range(a + half, a + HEAD_DIM), *range(b + half, b + HEAD_DIM)]
        return cols
    def shared_cols(base):
        x1, x2 = list(range(base, base + half)), list(range(base + half, base + HEAD_DIM))
        return x1 + x1 + x2 + x2
    o_k, o_v, o_qi = WIDTH, WIDTH + HEAD_DIM, WIDTH + 2 * HEAD_DIM
    o_ki, o_wi = o_qi + WIDTH, o_qi + WIDTH + HEAD_DIM
    last = [*range(o_v, o_v + HEAD_DIM), *range(o_wi, o_wi + HEADS)]
    last += [-1] * (LANES - len(last))
    order = pair_cols(0) + pair_cols(o_qi) + shared_cols(o_k) + shared_cols(o_ki) + last
    assert len(order) == DSA_IN_PAD
    return np.asarray(order, np.int32)


def _heads_to_rows(x):
    lane = lax.broadcasted_iota(jnp.int32, (1, LANES), 1)
    second = (lane // (HEAD_DIM // 2)) % 2 == 1
    rows = []
    for h in range(HEADS):
        pair = x[:, (h // 2) * LANES:(h // 2 + 1) * LANES]
        rows.append(jnp.where(second == (h % 2 == 1), pair, 0.0))
    return jnp.concatenate(rows, axis=0)


def _dsa_kernel(qq_ref, k_ref, ki_ref, vw_ref, wq_ref, cosq_ref, sinq_ref, cosk_ref, sink_ref,
                o_ref,
                k_s, ki_s, vt_s, sc_s, s_all, acc_s, *, topk, idx_scale):
    j = pl.program_id(1)
    tq = qq_ref.shape[0]
    seq = k_ref.shape[0]
    kc = min(KEY_CHUNK, seq)
    n_ch = (j * tq + tq + kc - 1) // kc
    inf = jnp.float32(jnp.inf)

    @pl.when(j == 0)
    def _():
        k_s[...] = _rope(k_ref[...], cosk_ref[...], sink_ref[...]).astype(BF16)
        ki_s[...] = _rope(ki_ref[...], cosk_ref[...], sink_ref[...]).astype(BF16)
        vwt = vw_ref[...].T
        row = lax.broadcasted_iota(jnp.int32, (V_ROWS, seq), 0)
        vt = jnp.where(row < HEAD_DIM, vwt[:V_ROWS], (row == HEAD_DIM).astype(F32))
        for c in range(seq // kc):
            vt_s[c] = vt[:, c * kc:(c + 1) * kc].astype(BF16)

    cosq = cosq_ref[...]
    sinq = sinq_ref[...]
    q_all = _heads_to_rows(_rope(qq_ref[:, 0:WIDTH], cosq, sinq)
                           * (HEAD_DIM ** -0.5)).astype(BF16)
    qi_all = _heads_to_rows(_rope(qq_ref[:, WIDTH:2 * WIDTH], cosq, sinq)).astype(BF16)

    sel_r = lax.broadcasted_iota(jnp.int32, (2 * SUBLANES, LANES), 0)
    sel_c = lax.broadcasted_iota(jnp.int32, (2 * SUBLANES, LANES), 1)
    sel = (sel_c == sel_r + HEAD_DIM).astype(BF16)
    w1, w2, w3 = _split3(wq_ref[...])
    w_t = (_dot_nt(sel, w1) + (_dot_nt(sel, w2) + _dot_nt(sel, w3))) * idx_scale

    qpos = j * tq + lax.broadcasted_iota(jnp.int32, (1, tq), 1)
    key_in_chunk = lax.broadcasted_iota(jnp.int32, (kc, tq), 0)

    def chunk_rows(c):
        return pl.ds(pl.multiple_of(c * kc, kc), kc)

    def paired_loop(step, carry):
        w = DSA_CHUNKS_PER_TRIP
        carry = lax.fori_loop(
            0, n_ch // w, lambda t, cr: step([w * t + i for i in range(w)], cr), carry)
        width = w // 2
        while width >= 1:
            base = (n_ch // (2 * width)) * (2 * width)
            carry = lax.fori_loop(
                0, (n_ch // width) % 2,
                lambda _, cr, base=base, width=width: step([base + i for i in range(width)], cr),
                carry)
            width //= 2
        return carry

    def score_step(cs, carry):
        rmax, rmin, n_ge0, n_gt0 = carry
        lg = [[_dot_nt(ki_s[chunk_rows(c), :], qi_all[h * tq:(h + 1) * tq])
               for h in range(HEADS)] for c in cs]
        for i, c in enumerate(cs):
            acc = w_t[0:1] * jnp.maximum(lg[i][0], 0.0)
            for h in range(1, HEADS):
                acc = acc + w_t[h:h + 1] * jnp.maximum(lg[i][h], 0.0)
            causal = (c * kc + key_in_chunk) <= qpos
            lo_fill = jnp.where(causal, acc, -inf)
            sc_s[chunk_rows(c), :] = lo_fill
            rmax = jnp.maximum(rmax, _fold(lo_fill, jnp.maximum))
            rmin = jnp.minimum(rmin, _fold(jnp.where(causal, acc, inf), jnp.minimum))
            n_ge0 = n_ge0 + _fold((lo_fill >= 0.0).astype(F32), jnp.add)
            n_gt0 = n_gt0 + _fold((lo_fill > 0.0).astype(F32), jnp.add)
        return rmax, rmin, n_ge0, n_gt0

    zeros8 = jnp.zeros((SUBLANES, tq), F32)
    rmax, rmin, n_ge0, n_gt0 = paired_loop(
        score_step,
        (jnp.full((SUBLANES, tq), -inf, F32), jnp.full((SUBLANES, tq), inf, F32), zeros8, zeros8))
    rmax = jnp.max(rmax, axis=0, keepdims=True)
    rmin = jnp.min(rmin, axis=0, keepdims=True)
    c_ge0 = jnp.sum(n_ge0, axis=0, keepdims=True)
    c_gt0 = jnp.sum(n_gt0, axis=0, keepdims=True)

    def count(cmp, x):
        xb = jnp.broadcast_to(x, (SUBLANES, tq))

        def body(c, accs):
            parts = sc_s[chunk_rows(c), :].reshape(kc // SUBLANES, SUBLANES, tq)
            accs = list(accs)
            for i in range(kc // SUBLANES):
                a = accs[i % COUNT_CHAINS]
                accs[i % COUNT_CHAINS] = jnp.where(cmp(parts[i], xb), a + 1.0, a)
            return tuple(accs)

        accs = lax.fori_loop(0, n_ch, body,
                             tuple(jnp.zeros((SUBLANES, tq), F32) for _ in range(COUNT_CHAINS)))
        return jnp.sum(_fold(jnp.concatenate(accs, axis=0), jnp.add), axis=0, keepdims=True)

    kf = jnp.float32(topk)
    n_causal = (qpos + 1).astype(F32)
    all_keys = n_causal < kf
    ge = lambda s, x: s >= x
    gt = lambda s, x: s > x
    zero = jnp.zeros((1, tq), F32)
    c_max = count(ge, rmax)
    top_tied = c_max >= kf
    zero_thr = (c_gt0 < kf) & (c_ge0 >= kf)
    above0 = c_gt0 >= kf
    lo = jnp.where(top_tied, rmax, jnp.where(zero_thr | above0, zero, rmin))
    c_lo = jnp.where(top_tied, c_max, jnp.where(zero_thr | above0, c_ge0, n_causal))
    hi = jnp.where(top_tied | above0, rmax, zero)
    done = all_keys | top_tied | zero_thr | (c_lo == kf) | (lo == hi)

    def search_cond(st):
        _, _, _, done_f, it = st
        return jnp.logical_and(jnp.min(done_f) < 0.5, it < SEARCH_ITERS_MAX)

    def search_body(st):
        lo, hi, c_lo, done_f, it = st
        for _ in range(SEARCH_UNROLL):
            active = done_f < 0.5
            mid = lo * 0.5 + hi * 0.5
            c = count(ge, mid)
            enough = c >= kf
            stuck = (mid == lo) | (mid == hi)
            lo = jnp.where(active & enough, mid, lo)
            hi = jnp.where(active & jnp.logical_not(enough), mid, hi)
            c_lo = jnp.where(active & enough, c, c_lo)
            done_f = jnp.where(active & ((c_lo == kf) | stuck), jnp.float32(1.0), done_f)
        return lo, hi, c_lo, done_f, it + SEARCH_UNROLL

    lo, hi, c_lo, _, _ = lax.while_loop(
        search_cond, search_body, (lo, hi, c_lo, done.astype(F32), jnp.int32(0)))
    lo = jnp.where(all_keys, -inf, lo)
    exact_k = all_keys | (c_lo == kf)
    no_ties = jnp.min(exact_k.astype(F32)) > 0.5

    @pl.when(no_ties)
    def _():
        def body(c, carry):
            s = sc_s[chunk_rows(c), :]
            causal = (c * kc + key_in_chunk) <= qpos
            sc_s[chunk_rows(c), :] = jnp.where((s >= lo) & causal, 0.0, MASK_BIAS)
            return carry
        lax.fori_loop(0, n_ch, body, 0)

    @pl.when(jnp.logical_not(no_ties))
    def _():
        def snap_body(c, acc):
            s = sc_s[chunk_rows(c), :]
            return jnp.minimum(acc, _fold(jnp.where(s >= lo, s, inf), jnp.minimum))
        thr = jnp.min(lax.fori_loop(0, n_ch, snap_body, jnp.full((SUBLANES, tq), inf, F32)),
                      axis=0, keepdims=True)
        need = kf - count(gt, thr)
        ki_ = lax.broadcasted_iota(jnp.int32, (kc, kc), 0)
        kj_ = lax.broadcasted_iota(jnp.int32, (kc, kc), 1)
        earlier = (kj_ < ki_).astype(BF16)
        ones = jnp.ones((2 * SUBLANES, kc), BF16)

        def body(c, run):
            s = sc_s[chunk_rows(c), :]
            causal = (c * kc + key_in_chunk) <= qpos
            eq = (s == thr).astype(BF16)
            rank = run + _dot(earlier, eq)
            take = (s > thr) | ((s == thr) & (rank < need))
            sc_s[chunk_rows(c), :] = jnp.where(take & causal, 0.0, MASK_BIAS)
            return run + _dot(ones, eq)[0:1]
        lax.fori_loop(0, n_ch, body, jnp.zeros((1, tq), F32))

    def logits_step(cs, m_acc):
        s = [[_dot_nt(k_s[chunk_rows(c), :], q_all[h * tq:(h + 1) * tq])
              for h in range(HEADS)] for c in cs]
        for i, c in enumerate(cs):
            bias = sc_s[chunk_rows(c), :]
            out = []
            for h in range(HEADS):
                sh = s[i][h] + bias
                s_all[chunk_rows(c), h * tq:(h + 1) * tq] = sh
                out.append(_fold(sh, jnp.maximum))
            m_acc = jnp.maximum(m_acc, jnp.concatenate(out, axis=1))
        return m_acc

    m = jnp.max(paired_loop(logits_step, jnp.full((SUBLANES, HEADS * tq), MASK_BIAS, F32)),
                axis=0, keepdims=True)

    def pv_step(cs, carry):
        for h in range(HEADS):
            cols = slice(h * tq, (h + 1) * tq)
            p = [jnp.exp(s_all[chunk_rows(c), cols] - m[:, cols]).astype(BF16) for c in cs]
            upd = _dot(vt_s[cs[0]], p[0])
            for i in range(1, len(cs)):
                upd = upd + _dot(vt_s[cs[i]], p[i])
            acc_s[:, cols] += upd
        return carry

    acc_s[...] = jnp.zeros_like(acc_s)
    paired_loop(pv_step, 0)
    out_t = acc_s[0:HEAD_DIM, :] / acc_s[HEAD_DIM:HEAD_DIM + 1, :]
    for p in range(PAIRS):
        pair = jnp.concatenate([out_t[:, 2 * p * tq:(2 * p + 1) * tq],
                                out_t[:, (2 * p + 1) * tq:(2 * p + 2) * tq]], axis=0)
        o_ref[:, p * LANES:(p + 1) * LANES] = pair.T.astype(o_ref.dtype)


def _dsa(zd, batch, seq, cos_t, sin_t):
    tq = min(Q_TILE, seq)
    nb = seq // tq
    topk = min(TOPK_MAX, seq // 4)
    kc = min(KEY_CHUNK, seq)
    idx_scale = HEADS ** -0.5 * HEAD_DIM ** -0.5
    kernel = functools.partial(_dsa_kernel, topk=topk, idx_scale=idx_scale)
    qcols = 2 * WIDTH // LANES
    return pl.pallas_call(
        kernel,
        grid=(batch, nb),
        in_specs=[
            pl.BlockSpec((tq, 2 * WIDTH), lambda b, j: (b * nb + j, 0)),
            pl.BlockSpec((seq, LANES), lambda b, j: (b, qcols)),
            pl.BlockSpec((seq, LANES), lambda b, j: (b, qcols + 1)),
            pl.BlockSpec((seq, LANES), lambda b, j: (b, qcols + 2)),
            pl.BlockSpec((tq, LANES), lambda b, j: (b * nb + j, qcols + 2)),
            pl.BlockSpec((tq, LANES), lambda b, j: (j, 0)),
            pl.BlockSpec((tq, LANES), lambda b, j: (j, 0)),
            pl.BlockSpec((seq, LANES), lambda b, j: (0, 0)),
            pl.BlockSpec((seq, LANES), lambda b, j: (0, 0)),
        ],
        out_specs=pl.BlockSpec((tq, WIDTH), lambda b, j: (b * nb + j, 0)),
        out_shape=jax.ShapeDtypeStruct((batch * seq, WIDTH), BF16),
        scratch_shapes=[
            pltpu.VMEM((seq, LANES), BF16),
            pltpu.VMEM((seq, LANES), BF16),
            pltpu.VMEM((seq // kc, V_ROWS, kc), BF16),
            pltpu.VMEM((seq, tq), F32),
            pltpu.VMEM((seq, HEADS * tq), F32),
            pltpu.VMEM((V_ROWS, HEADS * tq), F32),
        ],
        compiler_params=pltpu.CompilerParams(
            dimension_semantics=("arbitrary", "arbitrary"), vmem_limit_bytes=VMEM_LIMIT),
        name="dsa",
    )(zd, zd, zd, zd, zd, cos_t, sin_t, cos_t, sin_t)


def _merge_ffn_kernel(x_ref, ya_ref, yb_ref, zg_ref, wba_ref, wbb_ref, wout_ref, gffn_ref,
                      wup_ref, wdown_ref, gfin_ref, o_ref, *, final, hidden_chunk):
    d = x_ref.shape[1]
    pa = _dot(ya_ref[...], wba_ref[...])
    pb = _dot(yb_ref[...], wbb_ref[...])
    merged = _sigmoid(zg_ref[:, 0:d]) * pa + _sigmoid(zg_ref[:, d:2 * d]) * pb
    x1 = x_ref[...] + _dot(merged.astype(BF16), wout_ref[...])
    h2 = _rms(x1, gffn_ref[...]).astype(BF16)
    x2 = x1
    hidden = wup_ref.shape[1]
    for c in range(hidden // hidden_chunk):
        sl = slice(c * hidden_chunk, (c + 1) * hidden_chunk)
        u = jnp.maximum(_dot(h2, wup_ref[:, sl]), 0.0)
        x2 = x2 + _dot((u * u).astype(BF16), wdown_ref[sl, :])
    o_ref[...] = _rms(x2, gfin_ref[...]) if final else x2


def _merge_ffn(x2, ya, yb, zg, wba, wbb, wout, gffn, wup, wdown, gfin, tm, final):
    n, d = x2.shape
    const = lambda i: (0, 0)
    row = lambda i: (i, 0)
    resident = lambda arr: pl.BlockSpec(arr.shape, const, pipeline_mode=pl.Buffered(1))
    kernel = functools.partial(_merge_ffn_kernel, final=final,
                               hidden_chunk=min(1024, wup.shape[1]))
    return pl.pallas_call(
        kernel,
        grid=(n // tm,),
        in_specs=[
            pl.BlockSpec((tm, d), row),
            pl.BlockSpec((tm, WIDTH), row),
            pl.BlockSpec((tm, WIDTH), row),
            pl.BlockSpec((tm, 2 * d), row),
            resident(wba), resident(wbb), resident(wout),
            pl.BlockSpec((1, d), const),
            resident(wup), resident(wdown),
            pl.BlockSpec((1, d), const),
        ],
        out_specs=pl.BlockSpec((tm, d), row),
        out_shape=jax.ShapeDtypeStruct((n, d), F32),
        compiler_params=pltpu.CompilerParams(
            dimension_semantics=("arbitrary",), vmem_limit_bytes=VMEM_LIMIT),
        name="merge_ffn",
    )(x2, ya, yb, zg, wba, wbb, wout, gffn, wup, wdown, gfin)


def _rope_tables(seq):
    inv = 1.0 / (ROPE_THETA ** (jnp.arange(0, HEAD_DIM, 2, dtype=F32) / HEAD_DIM))
    ang = jnp.arange(seq, dtype=F32)[:, None] * inv[None, :]
    cos, sin = jnp.cos(ang), jnp.sin(ang)
    cos_t = jnp.concatenate([cos, cos, cos, cos], axis=1)
    sin_t = jnp.concatenate([-sin, -sin, sin, sin], axis=1)
    return cos_t, sin_t


def _pad_cols(a, n):
    return jnp.pad(a, ((0, 0), (0, n - a.shape[1])))


def _pad_rows(a, n):
    return jnp.pad(a, ((0, n - a.shape[0]), (0, 0)))


def kernel(x, g_mix, w_in, mu_shift, decay_bias, w_decay_up, iclr_bias, w_iclr_up, w_gate_up,
           k_k, k_a, r_k, gn_w, gn_b, w_branch, w_out, g_ffn, w_ffn_up, w_ffn_down, g_final):
    batch, seq, d = x.shape
    depth = g_mix.shape[0]
    n = batch * seq
    tm = 512
    assert seq % min(Q_TILE, seq) == 0 and seq % min(KEY_CHUNK, seq) == 0
    assert seq % min(RWKV_BLOCK, seq) == 0
    assert min(RWKV_BLOCK, seq) % RWKV_CHUNK == 0
    assert n % tm == 0
    cos_t, sin_t = _rope_tables(seq)
    row = lambda a: a.reshape(1, -1)
    xf = x.reshape(n, d)
    for l in range(depth):
        w = w_in[l]
        w_rwkv = _pad_cols(w[:, :RWKV_IN], RWKV_IN_PAD).astype(BF16)
        wd = _pad_cols(w[:, RWKV_IN:RWKV_IN + DSA_IN], DSA_IN + 1)
        order = _pair_layout()
        w_dsa = jnp.take(wd, jnp.asarray(np.where(order < 0, DSA_IN, order)),
                         axis=1).astype(BF16)
        w_gate = w[:, RWKV_IN + DSA_IN:].astype(BF16)
        zr, zd, zg = _inproj(xf, row(g_mix[l]), w_rwkv, w_dsa, w_gate, tm)

        wdec = _pad_rows(w_decay_up[l], LANES).astype(BF16)
        wicl = jnp.pad(w_iclr_up[l], ((DECAY_LORA, 0), (0, 0))).astype(BF16)
        wgate = _pad_rows(w_gate_up[l], GATE_LORA_PAD).astype(BF16)
        mu = _pad_cols(row(mu_shift[l]), RWKV_IN_PAD)
        ya = _rwkv(zr, batch, seq, mu, row(decay_bias[l]), wdec, row(iclr_bias[l]), wicl,
                   wgate, row(k_k[l]), row(k_a[l]), row(r_k[l]), row(gn_w[l]), row(gn_b[l]))
        yb = _dsa(zd, batch, seq, cos_t, sin_t)

        xf = _merge_ffn(xf, ya, yb, zg, w_branch[l, 0].astype(BF16),
                        w_branch[l, 1].astype(BF16), w_out[l].astype(BF16), row(g_ffn[l]),
                        w_ffn_up[l].astype(BF16), w_ffn_down[l].astype(BF16), row(g_final),
                        tm, final=(l == depth - 1))
    return xf.reshape(batch, seq, d)
```
